```python
import jax, jax.numpy as jnp
from jax import lax
import numpy as np

D_MODEL = 1024
BATCH = 16
SEQ = 2048
DEPTH = 2

MEM_LEN = 256
Q_BLOCK = 128
ROPE_THETA = 10000.0
NORM_EPS = 1e-6
FOX_HEADS = 6
FOX_DIM = 64
FORGET_BIAS_MEAN = 2.0
MLA_HEADS = 6
MLA_NOPE = 64
MLA_ROPE = 32
MLA_V = 64
MLA_Q_RANK = 256
MLA_KV_RANK = 128
DSA_HEADS = 4
DSA_DIM = 64
IDX_HEADS = 8
IDX_DIM = 32
TOPK_MAX = 256
XA_HEADS = 4
XA_DIM = 128
D_FF = 2816
IN_SIZES = (FOX_HEADS * FOX_DIM, FOX_HEADS * FOX_DIM, FOX_HEADS * FOX_DIM, FOX_HEADS,
            MLA_Q_RANK, MLA_KV_RANK, MLA_ROPE,
            DSA_HEADS * DSA_DIM, DSA_DIM, DSA_DIM, IDX_HEADS * IDX_DIM, IDX_DIM, IDX_HEADS)
D_IN = sum(IN_SIZES)
D_MIX = FOX_HEADS * FOX_DIM + MLA_HEADS * MLA_V + DSA_HEADS * DSA_DIM

kernel_name = "hybrid_fox_mla_dsa_macaron_block"


def _rmsnorm(x, g):
    xf = x.astype(jnp.float32)
    y = xf * lax.rsqrt(jnp.mean(xf * xf, axis=-1, keepdims=True) + NORM_EPS)
    return (y * g.astype(jnp.float32)).astype(x.dtype)


def _rope(x, pos):
    half = x.shape[-1] // 2
    inv = ROPE_THETA ** (-jnp.arange(half, dtype=jnp.float32) / half)
    ang = pos.astype(jnp.float32)[:, None] * inv[None, :]
    cos = jnp.cos(ang)[:, None, :]
    sin = jnp.sin(ang)[:, None, :]
    xf = x.astype(jnp.float32)
    x1, x2 = xf[..., :half], xf[..., half:]
    return jnp.concatenate([x1 * cos - x2 * sin, x2 * cos + x1 * sin], axis=-1).astype(x.dtype)


def _swiglu(h, wi, wo):
    gate, up = jnp.split(h @ wi, 2, axis=-1)
    return (jax.nn.silu(gate) * up) @ wo


def _split(a, sizes):
    offsets = [int(o) for o in np.cumsum(sizes)[:-1]]
    return jnp.split(a, offsets, axis=-1)


def _to_blocks(a):
    b, s = a.shape[0], a.shape[1]
    a = a.reshape((b, s // Q_BLOCK, Q_BLOCK) + a.shape[2:])
    return jnp.moveaxis(a, 1, 0)


def _from_blocks(a):
    a = jnp.moveaxis(a, 0, 1)
    return a.reshape((a.shape[0], a.shape[1] * a.shape[2]) + a.shape[3:])


def _causal_block_attention(q, k, v, cum_logf=None):
    s_len = q.shape[1]
    nb = s_len // Q_BLOCK
    scale = q.shape[-1] ** -0.5
    pos = jnp.arange(s_len)
    fk = None if cum_logf is None else jnp.swapaxes(cum_logf, 1, 2)

    def one_block(args):
        qb, pb, fb = args
        s = jnp.einsum('bqhd,bkhd->bhqk', qb, k).astype(jnp.float32) * scale
        if fk is not None:
            s = s + (jnp.swapaxes(fb, 1, 2)[..., :, None] - fk[:, :, None, :])
        s = jnp.where((pb[:, None] >= pos[None, :])[None, None], s, -jnp.inf)
        p = jax.nn.softmax(s, axis=-1).astype(v.dtype)
        return jnp.einsum('bhqk,bkhd->bqhd', p, v)

    fq = None if cum_logf is None else _to_blocks(cum_logf)
    out = lax.map(one_block, (_to_blocks(q), pos.reshape(nb, Q_BLOCK), fq))
    return _from_blocks(out)


def _dsa_attention(q, k, v, qi, ki, wi, n_sel):
    s_len = q.shape[1]
    nb = s_len // Q_BLOCK
    scale = q.shape[-1] ** -0.5
    pos = jnp.arange(s_len)
    gather = jax.vmap(lambda t, i: t[i])

    def one_block(args):
        qb, qib, wb, pb = args
        sc = jnp.einsum('bqhd,bkd->bqhk', qib, ki).astype(jnp.float32) * (IDX_DIM ** -0.5)
        iscore = jnp.einsum('bqhk,bqh->bqk', jax.nn.relu(sc), wb.astype(jnp.float32))
        iscore = jnp.where((pb[:, None] >= pos[None, :])[None], iscore, -jnp.inf)
        _, sel = lax.top_k(iscore, n_sel)
        valid = sel <= pb[None, :, None]
        k_sel = gather(k, sel)
        v_sel = gather(v, sel)
        s = jnp.einsum('bqhd,bqkd->bqhk', qb, k_sel).astype(jnp.float32) * scale
        s = jnp.where(valid[:, :, None, :], s, -jnp.inf)
        p = jax.nn.softmax(s, axis=-1).astype(v.dtype)
        return jnp.einsum('bqhk,bqkd->bqhd', p, v_sel)

    out = lax.map(one_block, (_to_blocks(q), _to_blocks(qi), _to_blocks(wi),
                              pos.reshape(nb, Q_BLOCK)))
    return _from_blocks(out)


def setup_inputs(seed: int = 0) -> dict:
    key = jax.random.key(seed)
    ks = iter(jax.random.split(key, 32))
    f32 = jnp.float32

    def w(shape, fan_in):
        return jax.random.normal(next(ks), shape, f32) * (fan_in ** -0.5)

    def gain(shape):
        return 1.0 + 0.05 * jax.random.normal(next(ks), shape, f32)

    L = DEPTH
    return {
        'x': jax.random.normal(next(ks), (BATCH, SEQ, D_MODEL), f32),
        'mem': jax.random.normal(next(ks), (BATCH, MEM_LEN, D_MODEL), f32),
        'ffn1_norm': gain((L, D_MODEL)),
        'ffn1_wi': w((L, D_MODEL, 2 * D_FF), D_MODEL),
        'ffn1_wo': w((L, D_FF, D_MODEL), D_FF),
        'mix_norm': gain((L, D_MODEL)),
        'w_in': w((L, D_MODEL, D_IN), D_MODEL),
        'b_forget': FORGET_BIAS_MEAN + 0.1 * jax.random.normal(next(ks), (L, FOX_HEADS), f32),
        'mla_q_norm': gain((L, MLA_Q_RANK)),
        'mla_w_uq': w((L, MLA_Q_RANK, MLA_HEADS * (MLA_NOPE + MLA_ROPE)), MLA_Q_RANK),
        'mla_kv_norm': gain((L, MLA_KV_RANK)),
        'mla_w_ukv': w((L, MLA_KV_RANK, MLA_HEADS * (MLA_NOPE + MLA_V)), MLA_KV_RANK),
        'w_out': w((L, D_MIX, D_MODEL), D_MIX),
        'xa_norm': gain((L, D_MODEL)),
        'mem_norm': gain((L, D_MODEL)),
        'xa_wq': w((L, D_MODEL, XA_HEADS * XA_DIM), D_MODEL),
        'xa_wkv': w((L, D_MODEL, 2 * XA_HEADS * XA_DIM), D_MODEL),
        'xa_wo': w((L, XA_HEADS * XA_DIM, D_MODEL), XA_HEADS * XA_DIM),
        'ffn2_norm': gain((L, D_MODEL)),
        'ffn2_wi': w((L, D_MODEL, 2 * D_FF), D_MODEL),
        'ffn2_wo': w((L, D_FF, D_MODEL), D_FF),
        'final_norm': gain((D_MODEL,)),
    }


def reference(x, mem, ffn1_norm, ffn1_wi, ffn1_wo, mix_norm, w_in, b_forget,
              mla_q_norm, mla_w_uq, mla_kv_norm, mla_w_ukv, w_out,
              xa_norm, mem_norm, xa_wq, xa_wkv, xa_wo,
              ffn2_norm, ffn2_wi, ffn2_wo, final_norm):
    b, s_len, _ = x.shape
    m_len = mem.shape[1]
    pos = jnp.arange(s_len)
    n_sel = min(TOPK_MAX, s_len // 4)

    for l in range(DEPTH):
        x = x + 0.5 * _swiglu(_rmsnorm(x, ffn1_norm[l]), ffn1_wi[l], ffn1_wo[l])

        h = _rmsnorm(x, mix_norm[l])
        (fq, fk, fv, ff, mcq, mckv, mkr,
         dq, dk, dv, dqi, dki, dwi) = _split(h @ w_in[l], IN_SIZES)

        log_f = jax.nn.log_sigmoid((ff + b_forget[l]).astype(jnp.float32))
        cum_logf = jnp.cumsum(log_f, axis=1)
        a_out = _causal_block_attention(
            fq.reshape(b, s_len, FOX_HEADS, FOX_DIM),
            fk.reshape(b, s_len, FOX_HEADS, FOX_DIM),
            fv.reshape(b, s_len, FOX_HEADS, FOX_DIM), cum_logf)

        qm = (_rmsnorm(mcq, mla_q_norm[l]) @ mla_w_uq[l]).reshape(
            b, s_len, MLA_HEADS, MLA_NOPE + MLA_ROPE)
        q_nope, q_rope = qm[..., :MLA_NOPE], qm[..., MLA_NOPE:]
        kvm = (_rmsnorm(mckv, mla_kv_norm[l]) @ mla_w_ukv[l]).reshape(
            b, s_len, MLA_HEADS, MLA_NOPE + MLA_V)
        k_nope, v_m = kvm[..., :MLA_NOPE], kvm[..., MLA_NOPE:]
        k_rope = _rope(mkr.reshape(b, s_len, 1, MLA_ROPE), pos)
        q_m = jnp.concatenate([q_nope, _rope(q_rope, pos)], axis=-1)
        k_m = jnp.concatenate(
            [k_nope, jnp.broadcast_to(k_rope, (b, s_len, MLA_HEADS, MLA_ROPE))], axis=-1)
        b_out = _causal_block_attention(q_m, k_m, v_m)

        q_c = _rope(dq.reshape(b, s_len, DSA_HEADS, DSA_DIM), pos)
        k_c = _rope(dk.reshape(b, s_len, 1, DSA_DIM), pos)[:, :, 0]
        qi = _rope(dqi.reshape(b, s_len, IDX_HEADS, IDX_DIM), pos)
        ki = _rope(dki.reshape(b, s_len, 1, IDX_DIM), pos)[:, :, 0]
        wi = dwi * (IDX_HEADS ** -0.5)
        c_out = _dsa_attention(q_c, k_c, dv, qi, ki, wi, n_sel)

        mixed = jnp.concatenate([a_out.reshape(b, s_len, -1), b_out.reshape(b, s_len, -1),
                                 c_out.reshape(b, s_len, -1)], axis=-1)
        x = x + mixed @ w_out[l]

        hq = _rmsnorm(x, xa_norm[l])
        hm = _rmsnorm(mem, mem_norm[l])
        qx = (hq @ xa_wq[l]).reshape(b, s_len, XA_HEADS, XA_DIM)
        kx, vx = jnp.split((hm @ xa_wkv[l]).reshape(b, m_len, 2 * XA_HEADS, XA_DIM), 2, axis=2)
        sx = jnp.einsum('bqhd,bmhd->bhqm', qx, kx).astype(jnp.float32) * (XA_DIM ** -0.5)
        px = jax.nn.softmax(sx, axis=-1).astype(vx.dtype)
        ox = jnp.einsum('bhqm,bmhd->bqhd', px, vx).reshape(b, s_len, XA_HEADS * XA_DIM)
        x = x + ox @ xa_wo[l]

        x = x + 0.5 * _swiglu(_rmsnorm(x, ffn2_norm[l]), ffn2_wi[l], ffn2_wo[l])

    return _rmsnorm(x, final_norm)
```

```python
import functools

import jax
import jax.numpy as jnp
import numpy as np
from jax import lax
from jax.experimental import pallas as pl
from jax.experimental.pallas import tpu as pltpu

F32 = jnp.float32
BF16 = jnp.bfloat16

ROPE_THETA = 10000.0
NORM_EPS = 1e-6
FOX_HEADS, FOX_DIM = 6, 64
MLA_HEADS, MLA_NOPE, MLA_ROPE, MLA_V = 6, 64, 32, 64
MLA_Q_RANK, MLA_KV_RANK = 256, 128
DSA_HEADS, DSA_DIM = 4, 64
IDX_HEADS, IDX_DIM = 8, 32
TOPK_MAX = 256
XA_HEADS, XA_DIM = 4, 128

LANES = 128
VMEM_LIMIT = 56 * 1024 * 1024
NEG = -1e30
INT_MIN = -(2 ** 31)

O_FQ, O_FK, O_FV, O_FF = 0, 384, 768, 1152
O_MCQ, O_MCKV, O_MKR = 1280, 1536, 1664
O_DQ, O_DK, O_DV, O_DQI, O_DKI, O_DWI = 1792, 2048, 2176, 2304, 2560, 2688
N_PROJ = 2816

FFN_TM, FFN_TF = 1024, 256
PROJ_TM = 256
ATT_T = 256
MIX_TM = 512


def _dot(a, b):
    return jnp.dot(a, b, preferred_element_type=F32)


def _dot_nt(a, b):
    return lax.dot_general(a, b, (((1,), (1,)), ((), ())), preferred_element_type=F32)


def _rmsnorm(x, g):
    return x * lax.rsqrt(jnp.mean(x * x, axis=-1, keepdims=True) + NORM_EPS) * g


def _vec_spec(width, layer):
    return pl.BlockSpec((None, 1, width), lambda *_: (layer, 0, 0))


def _params(*sem):
    return pltpu.CompilerParams(dimension_semantics=sem, vmem_limit_bytes=VMEM_LIMIT)


def _ffn_kernel(x_ref, g_ref, wg_ref, wu_ref, wo_ref, fg_ref, o_ref, h_ref, acc_ref, *, final_norm):
    j = pl.program_id(1)

    @pl.when(j == 0)
    def _():
        h_ref[...] = _rmsnorm(x_ref[...], g_ref[...]).astype(BF16)
        acc_ref[...] = jnp.zeros_like(acc_ref)

    h = h_ref[...]
    gate = _dot(h, wg_ref[...])
    up = _dot(h, wu_ref[...])
    act = (gate * jax.nn.sigmoid(gate) * up).astype(BF16)
    acc_ref[...] += _dot(act, wo_ref[...])

    @pl.when(j == pl.num_programs(1) - 1)
    def _():
        y = x_ref[...] + 0.5 * acc_ref[...]
        if final_norm:
            y = _rmsnorm(y, fg_ref[...])
        o_ref[...] = y


def _ffn(x2d, gain, wi, wo, layer, final_gain=None):
    m, d = x2d.shape
    f = wo.shape[1]
    nf = f // FFN_TF
    fg = gain if final_gain is None else final_gain
    return pl.pallas_call(
        functools.partial(_ffn_kernel, final_norm=final_gain is not None),
        grid=(m // FFN_TM, nf),
        in_specs=[
            pl.BlockSpec((FFN_TM, d), lambda i, j: (i, 0)),
            _vec_spec(d, layer),
            pl.BlockSpec((None, d, FFN_TF), lambda i, j: (layer, 0, j)),
            pl.BlockSpec((None, d, FFN_TF), lambda i, j: (layer, 0, j + nf)),
            pl.BlockSpec((None, FFN_TF, d), lambda i, j: (layer, j, 0)),
            _vec_spec(d, 0 if final_gain is not None else layer),
        ],
        out_specs=pl.BlockSpec((FFN_TM, d), lambda i, j: (i, 0)),
        out_shape=jax.ShapeDtypeStruct((m, d), F32),
        scratch_shapes=[pltpu.VMEM((FFN_TM, d), BF16), pltpu.VMEM((FFN_TM, d), F32)],
        compiler_params=_params("parallel", "arbitrary"),
        name="ffn",
    )(x2d, gain, wi, wi, wo, fg)


def _rope128(x, cos, sin, half):
    lane = lax.broadcasted_iota(jnp.int32, x.shape, 1)
    first = (lane % (2 * half)) < half
    partner = jnp.where(first, pltpu.roll(x, LANES - half, 1), pltpu.roll(x, half, 1))
    return x * cos + partner * sin


def _proj_kernel(x_ref, g_ref, win_ref, bf_ref, gq_ref, wuq_ref, gkv_ref, wukv_ref,
                 c32_ref, s32_ref, c64_ref, s64_ref,
                 fq_ref, fk_ref, fv_ref, cq_ref, ck_ref,
                 mq_ref, mk_ref, mv_ref,
                 dq_ref, dk_ref, dv_ref, qi_ref, ki_ref, wi_ref,
                 carry_ref):
    j = pl.program_id(1)
    tm = x_ref.shape[1]

    @pl.when(j == 0)
    def _():
        carry_ref[...] = jnp.zeros_like(carry_ref)

    h = _rmsnorm(x_ref[0], g_ref[...]).astype(BF16)
    p = _dot(h, win_ref[...])

    def grp(off, width=LANES):
        return p[:, off:off + width]

    c32, s32 = c32_ref[...], s32_ref[...]
    c64, s64 = c64_ref[...], s64_ref[...]

    fq_ref[0] = (grp(O_FQ, 384) * (FOX_DIM ** -0.5)).astype(BF16)
    fk_ref[0] = grp(O_FK, 384).astype(BF16)
    fv_ref[0] = grp(O_FV, 384).astype(BF16)

    z = grp(O_FF) + bf_ref[...]
    logf = jnp.minimum(z, 0.0) - jnp.log(1.0 + jnp.exp(-jnp.abs(z)))
    hi = logf.astype(BF16)
    r1 = logf - hi.astype(F32)
    mid = r1.astype(BF16)
    lo = (r1 - mid.astype(F32)).astype(BF16)
    row = lax.broadcasted_iota(jnp.int32, (tm, tm), 0)
    col = lax.broadcasted_iota(jnp.int32, (tm, tm), 1)
    tri = jnp.where(row >= col, 1.0, 0.0).astype(BF16)
    cum = (_dot(tri, hi) + _dot(tri, mid)) + _dot(tri, lo) + carry_ref[0:1, :]
    carry_ref[0:1, :] = cum[tm - 1:tm, :]
    cq_ref[0] = cum
    ck_ref[0, 0] = cum.T[0:8, :]

    qn = _rmsnorm(grp(O_MCQ, MLA_Q_RANK), gq_ref[...]).astype(BF16)
    qm = _dot(qn, wuq_ref[...]) * ((MLA_NOPE + MLA_ROPE) ** -0.5)
    kvn = _rmsnorm(grp(O_MCKV, MLA_KV_RANK), gkv_ref[...]).astype(BF16)
    kvm = _dot(kvn, wukv_ref[...])
    krt = _rope128(grp(O_MKR), c32, s32, MLA_ROPE // 2).astype(BF16)
    for pr in range(MLA_HEADS // 2):
        mq_ref[0, :, 256 * pr:256 * pr + 128] = qm[:, 256 * pr:256 * pr + 128].astype(BF16)
        mq_ref[0, :, 256 * pr + 128:256 * pr + 256] = _rope128(
            qm[:, 256 * pr + 128:256 * pr + 256], c32, s32, MLA_ROPE // 2).astype(BF16)
        mk_ref[0, :, 256 * pr:256 * pr + 128] = kvm[:, 128 * pr:128 * pr + 128].astype(BF16)
        mk_ref[0, :, 256 * pr + 128:256 * pr + 256] = krt
    mv_ref[0] = kvm[:, 384:768].astype(BF16)

    for gi in range(2):
        dq_ref[0, :, 128 * gi:128 * gi + 128] = (
            _rope128(grp(O_DQ + 128 * gi), c64, s64, DSA_DIM // 2) * (DSA_DIM ** -0.5)).astype(BF16)
        qi_ref[0, :, 128 * gi:128 * gi + 128] = (
            _rope128(grp(O_DQI + 128 * gi), c32, s32, IDX_DIM // 2) * (IDX_DIM ** -0.5)).astype(BF16)
    dk_ref[0] = _rope128(grp(O_DK), c64, s64, DSA_DIM // 2).astype(BF16)
    dv_ref[0] = grp(O_DV).astype(BF16)
    ki_ref[0] = _rope128(grp(O_DKI), c32, s32, IDX_DIM // 2).astype(BF16)
    wi_ref[0] = grp(O_DWI) * (IDX_HEADS ** -0.5)


def _proj(x, gain, win, bfp, gq, wuq, gkv, wukv, tabs, layer):
    b, s, d = x.shape
    tm = PROJ_TM
    nt = s // tm
    tok = lambda w, dt: jax.ShapeDtypeStruct((b, s, w), dt)
    tspec = lambda w: pl.BlockSpec((1, tm, w), lambda bi, j: (bi, j, 0))
    full = lambda shape: pl.BlockSpec((None,) + shape, lambda bi, j: (layer,) + (0,) * len(shape))
    vec = lambda w: _vec_spec(w, layer)
    tab = pl.BlockSpec((tm, LANES), lambda bi, j: (j, 0))
    out_shape = [tok(384, BF16), tok(384, BF16), tok(384, BF16), tok(LANES, F32),
                 jax.ShapeDtypeStruct((b, nt, 8, tm), F32),
                 tok(768, BF16), tok(768, BF16), tok(384, BF16),
                 tok(256, BF16), tok(LANES, BF16), tok(LANES, BF16),
                 tok(256, BF16), tok(LANES, BF16), tok(LANES, F32)]
    out_specs = [tspec(384), tspec(384), tspec(384), tspec(LANES),
                 pl.BlockSpec((1, 1, 8, tm), lambda bi, j: (bi, j, 0, 0)),
                 tspec(768), tspec(768), tspec(384),
                 tspec(256), tspec(LANES), tspec(LANES),
                 tspec(256), tspec(LANES), tspec(LANES)]
    return pl.pallas_call(
        _proj_kernel,
        grid=(b, nt),
        in_specs=[tspec(d), vec(d), full((d, N_PROJ)), vec(LANES),
                  vec(MLA_Q_RANK), full((MLA_Q_RANK, 768)), vec(MLA_KV_RANK), full((MLA_KV_RANK, 768)),
                  tab, tab, tab, tab],
        out_specs=out_specs,
        out_shape=out_shape,
        scratch_shapes=[pltpu.VMEM((8, LANES), F32)],
        compiler_params=_params("arbitrary", "arbitrary"),
        name="proj",
    )(x, gain, win, bfp, gq, wuq, gkv, wukv, *tabs)


def _softmax_step(s, m, l, acc, v):
    m_new = jnp.maximum(m, jnp.max(s, axis=-1, keepdims=True))
    alpha = jnp.exp(m - m_new)
    pe = jnp.exp(s - m_new)
    l = alpha * l + jnp.sum(pe, axis=-1, keepdims=True)
    acc = alpha * acc + _dot(pe.astype(BF16), v)
    return m_new, l, acc


def _attn_kernel(*refs, n_pairs, kw, has_bias):
    if has_bias:
        q_ref, k_ref, v_ref, cq_ref, ck_ref, o_ref = refs
    else:
        q_ref, k_ref, v_ref, o_ref = refs
    i = pl.program_id(1)
    t = q_ref.shape[1]
    lane = lax.broadcasted_iota(jnp.int32, (t, kw), 1)
    lane_o = lax.broadcasted_iota(jnp.int32, (t, LANES), 1)
    row = lax.broadcasted_iota(jnp.int32, (t, t), 0)
    col = lax.broadcasted_iota(jnp.int32, (t, t), 1)
    causal = row >= col

    for pr in range(n_pairs):
        qs = q_ref[0, :, kw * pr:kw * (pr + 1)]
        qh, cqh = [], []
        for e in range(2):
            sel = (lane >= 64 * e) & (lane < 64 * e + 64)
            if kw == 256:
                sel = sel | ((lane >= 128 + 32 * e) & (lane < 160 + 32 * e))
            qh.append(jnp.where(sel, qs, jnp.zeros_like(qs)))
            if has_bias:
                hd = 2 * pr + e
                cqh.append(cq_ref[0, :, hd:hd + 1])

        def chunk(c, carry, diag):
            start = pl.multiple_of(c * t, t)
            ks = k_ref[0, pl.ds(start, t), kw * pr:kw * (pr + 1)]
            vs = v_ref[0, pl.ds(start, t), LANES * pr:LANES * (pr + 1)]
            out = []
            for e in range(2):
                m, l, acc = carry[e]
                s = _dot_nt(qh[e], ks)
                if has_bias:
                    hd = 2 * pr + e
                    s = s + (cqh[e] - ck_ref[0, c, hd:hd + 1, :])
                if diag:
                    s = jnp.where(causal, s, NEG)
                out.append(_softmax_step(s, m, l, acc, vs))
            return tuple(out)

        init = tuple((jnp.full((t, 1), NEG, F32), jnp.zeros((t, 1), F32), jnp.zeros((t, LANES), F32))
                     for _ in range(2))
        carry = lax.fori_loop(0, i, lambda c, cr: chunk(c, cr, False), init)
        carry = chunk(i, carry, True)
        o0 = carry[0][2] / carry[0][1]
        o1 = carry[1][2] / carry[1][1]
        o_ref[0, :, LANES * pr:LANES * (pr + 1)] = jnp.where(lane_o < 64, o0, o1).astype(BF16)


def _attn(q, k, v, cq=None, ck=None, *, kw, name):
    b, s, qw = q.shape
    n_pairs = qw // kw
    t = ATT_T
    has_bias = cq is not None
    in_specs = [pl.BlockSpec((1, t, qw), lambda bi, i: (bi, i, 0)),
                pl.BlockSpec((1, s, qw), lambda bi, i: (bi, 0, 0)),
                pl.BlockSpec((1, s, n_pairs * LANES), lambda bi, i: (bi, 0, 0))]
    args = [q, k, v]
    if has_bias:
        in_specs += [pl.BlockSpec((1, t, LANES), lambda bi, i: (bi, i, 0)),
                     pl.BlockSpec((1, s // t, 8, t), lambda bi, i: (bi, 0, 0, 0))]
        args += [cq, ck]
    return pl.pallas_call(
        functools.partial(_attn_kernel, n_pairs=n_pairs, kw=kw, has_bias=has_bias),
        grid=(b, s // t),
        in_specs=in_specs,
        out_specs=pl.BlockSpec((1, t, n_pairs * LANES), lambda bi, i: (bi, i, 0)),
        out_shape=jax.ShapeDtypeStruct((b, s, n_pairs * LANES), BF16),
        compiler_params=_params("parallel", "arbitrary"),
        name=name,
    )(*args)


def _sortable(x):
    bits = lax.bitcast_convert_type(x + 0.0, jnp.int32)
    return jnp.where(bits < 0, bits ^ 0x7FFFFFFF, bits)


def _dsa_kernel(q_ref, k_ref, v_ref, qi_ref, ki_ref, wi_ref, o_ref, key_ref, bias_ref, *, n_sel):
    i = pl.program_id(1)
    t = q_ref.shape[1]
    nc = i + 1
    lane = lax.broadcasted_iota(jnp.int32, (t, LANES), 1)
    row = lax.broadcasted_iota(jnp.int32, (t, t), 0)
    col = lax.broadcasted_iota(jnp.int32, (t, t), 1)

    qi = qi_ref[0]
    wi = wi_ref[0]
    qih, wih = [], []
    for hd in range(IDX_HEADS):
        g, e = divmod(hd, LANES // IDX_DIM)
        grp = qi[:, LANES * g:LANES * (g + 1)]
        qih.append(jnp.where((lane >= IDX_DIM * e) & (lane < IDX_DIM * (e + 1)), grp, jnp.zeros_like(grp)))
        wih.append(wi[:, hd:hd + 1])

    def score_chunk(c, _):
        start = pl.multiple_of(c * t, t)
        kic = ki_ref[0, pl.ds(start, t), :]
        isc = jnp.zeros((t, t), F32)
        for hd in range(IDX_HEADS):
            isc = isc + jnp.maximum(_dot_nt(qih[hd], kic), 0.0) * wih[hd]
        visible = (col - row) <= (i - c) * t
        key_ref[c] = jnp.where(visible, _sortable(isc), INT_MIN)
        return 0

    lax.fori_loop(0, nc, score_chunk, 0)

    def count(pred):
        def body(c, cnt):
            kc = key_ref[c]
            idx = col + c * t
            return cnt + jnp.sum(jnp.where(pred(kc, idx), 1.0, 0.0), axis=-1, keepdims=True)
        return lax.fori_loop(0, nc, body, jnp.zeros((t, 1), F32))

    ge0 = count(lambda kc, idx: kc >= 0)
    thr = jnp.where(ge0 >= n_sel, 0, INT_MIN).astype(jnp.int32)

    def thr_step(b, thr):
        cand = thr | lax.shift_left(jnp.int32(1), 30 - b)
        cnt = count(lambda kc, idx: kc >= cand)
        return jnp.where(cnt >= n_sel, cand, thr)

    thr = lax.fori_loop(0, 31, thr_step, thr)

    need = n_sel - count(lambda kc, idx: kc > thr)

    def cut_step(b, cut):
        cand = cut | lax.shift_left(jnp.int32(1), 10 - b)
        cnt = count(lambda kc, idx: (kc == thr) & (idx < cand))
        return jnp.where(cnt < need, cand, cut)

    nbits = max(1, int(np.ceil(np.log2(k_ref.shape[1]))))
    cut = lax.fori_loop(11 - nbits, 11, cut_step, jnp.zeros((t, 1), jnp.int32))

    def bias_chunk(c, _):
        kc = key_ref[c]
        idx = col + c * t
        visible = (col - row) <= (i - c) * t
        keep = visible & ((kc > thr) | ((kc == thr) & (idx <= cut)))
        bias_ref[c] = jnp.where(keep, 0.0, NEG)
        return 0

    lax.fori_loop(0, nc, bias_chunk, 0)

    q = q_ref[0]
    qh = []
    for hd in range(DSA_HEADS):
        g, e = divmod(hd, 2)
        grp = q[:, LANES * g:LANES * (g + 1)]
        qh.append(jnp.where((lane >= 64 * e) & (lane < 64 * (e + 1)), grp, jnp.zeros_like(grp)))

    def att_chunk(c, carry):
        start = pl.multiple_of(c * t, t)
        ks = k_ref[0, pl.ds(start, t), :]
        vs = v_ref[0, pl.ds(start, t), :]
        bias = bias_ref[c]
        out = []
        for hd in range(DSA_HEADS):
            m, l, acc = carry[hd]
            out.append(_softmax_step(_dot_nt(qh[hd], ks) + bias, m, l, acc, vs))
        return tuple(out)

    init = tuple((jnp.full((t, 1), NEG, F32), jnp.zeros((t, 1), F32), jnp.zeros((t, LANES), F32))
                 for _ in range(DSA_HEADS))
    carry = lax.fori_loop(0, nc, att_chunk, init)
    for g in range(DSA_HEADS // 2):
        o0 = carry[2 * g][2] / carry[2 * g][1]
        o1 = carry[2 * g + 1][2] / carry[2 * g + 1][1]
        o_ref[0, :, LANES * g:LANES * (g + 1)] = jnp.where(lane < 64, o0, o1).astype(BF16)


def _dsa(q, k, v, qi, ki, wi, n_sel):
    b, s, _ = q.shape
    t = ATT_T
    tile = lambda w: pl.BlockSpec((1, t, w), lambda bi, i: (bi, i, 0))
    seq = lambda w: pl.BlockSpec((1, s, w), lambda bi, i: (bi, 0, 0))
    return pl.pallas_call(
        functools.partial(_dsa_kernel, n_sel=n_sel),
        grid=(b, s // t),
        in_specs=[tile(256), seq(LANES), seq(LANES), tile(256), seq(LANES), tile(LANES)],
        out_specs=tile(256),
        out_shape=jax.ShapeDtypeStruct((b, s, 256), BF16),
        scratch_shapes=[pltpu.VMEM((s // t, t, t), jnp.int32), pltpu.VMEM((s // t, t, t), F32)],
        compiler_params=_params("parallel", "arbitrary"),
        name="dsa",
    )(q, k, v, qi, ki, wi)


def _memkv_kernel(m_ref, g_ref, w_ref, k_ref, v_ref):
    hm = _rmsnorm(m_ref[0], g_ref[...]).astype(BF16)
    kv = _dot(hm, w_ref[...])
    half = kv.shape[1] // 2
    k_ref[0] = kv[:, :half].astype(BF16)
    v_ref[0] = kv[:, half:].astype(BF16)


def _memkv(mem, gain, wkv, layer):
    b, ml, d = mem.shape
    n = XA_HEADS * XA_DIM
    return pl.pallas_call(
        _memkv_kernel,
        grid=(b,),
        in_specs=[pl.BlockSpec((1, ml, d), lambda bi: (bi, 0, 0)),
                  _vec_spec(d, layer),
                  pl.BlockSpec((None, d, 2 * n), lambda bi: (layer, 0, 0))],
        out_specs=[pl.BlockSpec((1, ml, n), lambda bi: (bi, 0, 0))] * 2,
        out_shape=[jax.ShapeDtypeStruct((b, ml, n), BF16)] * 2,
        compiler_params=_params("parallel"),
        name="memkv",
    )(mem, gain, wkv)


def _mix_kernel(x_ref, a_ref, b_ref, c_ref, wout_ref, g_ref, wq_ref, kx_ref, vx_ref, wo_ref, o_ref):
    mixed = jnp.concatenate([a_ref[0], b_ref[0], c_ref[0]], axis=-1)
    x1 = x_ref[0] + _dot(mixed, wout_ref[...])
    hq = _rmsnorm(x1, g_ref[...]).astype(BF16)
    q = (_dot(hq, wq_ref[...]) * (XA_DIM ** -0.5)).astype(BF16)
    outs = []
    for hd in range(XA_HEADS):
        sl = slice(XA_DIM * hd, XA_DIM * (hd + 1))
        s = _dot_nt(q[:, sl], kx_ref[0, :, sl])
        pe = jnp.exp(s - jnp.max(s, axis=-1, keepdims=True))
        o = _dot(pe.astype(BF16), vx_ref[0, :, sl]) / jnp.sum(pe, axis=-1, keepdims=True)
        outs.append(o.astype(BF16))
    o_ref[0] = x1 + _dot(jnp.concatenate(outs, axis=-1), wo_ref[...])


def _mix_xattn(x, a, bm, c, wout, gain, wq, kx, vx, wo, layer):
    b, s, d = x.shape
    tm = MIX_TM
    n = XA_HEADS * XA_DIM
    ml = kx.shape[1]
    tile = lambda w: pl.BlockSpec((1, tm, w), lambda bi, j: (bi, j, 0))
    full = lambda r, cdim: pl.BlockSpec((None, r, cdim), lambda bi, j: (layer, 0, 0))
    mem = pl.BlockSpec((1, ml, n), lambda bi, j: (bi, 0, 0))
    return pl.pallas_call(
        _mix_kernel,
        grid=(b, s // tm),
        in_specs=[tile(d), tile(a.shape[2]), tile(bm.shape[2]), tile(c.shape[2]), full(d, d),
                  _vec_spec(d, layer), full(d, n), mem, mem, full(n, d)],
        out_specs=tile(d),
        out_shape=jax.ShapeDtypeStruct((b, s, d), F32),
        compiler_params=_params("parallel", "parallel"),
        name="mix_xattn",
    )(x, a, bm, c, wout, gain, wq, kx, vx, wo)


def _rope_tables(s, d):
    half = d // 2
    inv = ROPE_THETA ** (-jnp.arange(half, dtype=F32) / half)
    ang = jnp.arange(s).astype(F32)[:, None] * inv[None, :]
    cos, sin = jnp.cos(ang), jnp.sin(ang)
    reps = LANES // d
    return (jnp.tile(jnp.concatenate([cos, cos], -1), (1, reps)),
            jnp.tile(jnp.concatenate([-sin, sin], -1), (1, reps)))


def _layout_w_in(w_in):
    sizes = (384, 384, 384, FOX_HEADS, MLA_Q_RANK, MLA_KV_RANK, MLA_ROPE,
             DSA_HEADS * DSA_DIM, DSA_DIM, DSA_DIM, IDX_HEADS * IDX_DIM, IDX_DIM, IDX_HEADS)
    offs = np.concatenate([[0], np.cumsum(sizes)])
    fq, fk, fv, ff, mcq, mckv, mkr, dq, dk, dv, dqi, dki, dwi = (
        w_in[:, :, offs[n]:offs[n + 1]] for n in range(len(sizes)))
    zeros = lambda w: jnp.zeros(w_in.shape[:2] + (w,), w_in.dtype)
    pad = lambda a: jnp.concatenate([a, zeros(LANES - a.shape[2])], -1)
    out = jnp.concatenate([
        fq, fk, fv, pad(ff), mcq, mckv,
        jnp.concatenate([mkr, mkr, zeros(64)], -1),
        dq, jnp.concatenate([dk, dk], -1), jnp.concatenate([dv, dv], -1),
        dqi, jnp.concatenate([dki] * 4, -1), pad(dwi)], -1)
    assert out.shape[2] == N_PROJ
    return out.astype(BF16)


def _layout_w_uq(w_uq):
    hw = MLA_NOPE + MLA_ROPE
    cols = []
    for pr in range(MLA_HEADS // 2):
        h0, h1 = 2 * pr, 2 * pr + 1
        cols += [w_uq[:, :, hw * h0:hw * h0 + MLA_NOPE], w_uq[:, :, hw * h1:hw * h1 + MLA_NOPE],
                 w_uq[:, :, hw * h0 + MLA_NOPE:hw * (h0 + 1)], w_uq[:, :, hw * h1 + MLA_NOPE:hw * (h1 + 1)],
                 jnp.zeros(w_uq.shape[:2] + (64,), w_uq.dtype)]
    return jnp.concatenate(cols, -1).astype(BF16)


def _layout_w_ukv(w_ukv):
    hw = MLA_NOPE + MLA_V
    kn = [w_ukv[:, :, hw * hd:hw * hd + MLA_NOPE] for hd in range(MLA_HEADS)]
    vv = [w_ukv[:, :, hw * hd + MLA_NOPE:hw * (hd + 1)] for hd in range(MLA_HEADS)]
    return jnp.concatenate(kn + vv, -1).astype(BF16)


def kernel(x, mem, ffn1_norm, ffn1_wi, ffn1_wo, mix_norm, w_in, b_forget, mla_q_norm, mla_w_uq,
           mla_kv_norm, mla_w_ukv, w_out, xa_norm, mem_norm, xa_wq, xa_wkv, xa_wo,
           ffn2_norm, ffn2_wi, ffn2_wo, final_norm):
    b, s, d = x.shape
    depth = w_in.shape[0]
    assert s % ATT_T == 0 and s % MIX_TM == 0 and (b * s) % FFN_TM == 0 and s <= 2048
    n_sel = min(TOPK_MAX, s // 4)
    assert PROJ_TM == ATT_T and n_sel <= ATT_T

    win = _layout_w_in(w_in)
    wuq = _layout_w_uq(mla_w_uq)
    wukv = _layout_w_ukv(mla_w_ukv)
    rows = lambda v: v.reshape(-1, 1, v.shape[-1])
    bfp = rows(jnp.concatenate([b_forget, jnp.zeros((depth, LANES - FOX_HEADS), F32)], -1))
    ffn1_norm, mix_norm, mla_q_norm, mla_kv_norm, xa_norm, mem_norm, ffn2_norm, final_norm = map(
        rows, (ffn1_norm, mix_norm, mla_q_norm, mla_kv_norm, xa_norm, mem_norm, ffn2_norm, final_norm))
    tabs = _rope_tables(s, IDX_DIM) + _rope_tables(s, DSA_DIM)
    w1i, w1o = ffn1_wi.astype(BF16), ffn1_wo.astype(BF16)
    w2i, w2o = ffn2_wi.astype(BF16), ffn2_wo.astype(BF16)
    wout, wq, wkv, wo = (w.astype(BF16) for w in (w_out, xa_wq, xa_wkv, xa_wo))

    for layer in range(depth):
        x = _ffn(x.reshape(b * s, d), ffn1_norm, w1i, w1o, layer).reshape(b, s, d)
        (fq, fk, fv, cq, ck, mq, mk, mv, dq, dk, dv, qi, ki, wi) = _proj(
            x, mix_norm, win, bfp, mla_q_norm, wuq, mla_kv_norm, wukv, tabs, layer)
        a_out = _attn(fq, fk, fv, cq, ck, kw=LANES, name="fox_attn")
        b_out = _attn(mq, mk, mv, kw=2 * LANES, name="mla_attn")
        c_out = _dsa(dq, dk, dv, qi, ki, wi, n_sel)
        kx, vx = _memkv(mem, mem_norm, wkv, layer)
        x = _mix_xattn(x, a_out, b_out, c_out, wout, xa_norm, wq, kx, vx, wo, layer)
        last = layer == depth - 1
        x = _ffn(x.reshape(b * s, d), ffn2_norm, w2i, w2o, layer,
                 final_gain=final_norm if last else None).reshape(b, s, d)
    return x
```

```python
import functools

import jax
import jax.numpy as jnp
import numpy as np
from jax import lax
from jax.experimental import pallas as pl
from jax.experimental.pallas import tpu as pltpu

F32 = jnp.float32
BF16 = jnp.bfloat16

ROPE_THETA = 10000.0
NORM_EPS = 1e-6
FOX_HEADS, FOX_DIM = 6, 64
MLA_HEADS, MLA_NOPE, MLA_ROPE, MLA_V = 6, 64, 32, 64
MLA_Q_RANK, MLA_KV_RANK = 256, 128
DSA_HEADS, DSA_DIM = 4, 64
IDX_HEADS, IDX_DIM = 8, 32
TOPK_MAX = 256
XA_HEADS, XA_DIM = 4, 128

LANES = 128
VMEM_LIMIT = 56 * 1024 * 1024
NEG = -1e30
INT_MIN = -(2 ** 31)

O_FQ, O_FK, O_FV, O_FF = 0, 384, 768, 1152
O_MCQ, O_MCKV, O_MKR = 1280, 1536, 1664
O_DQ, O_DK, O_DV, O_DQI, O_DKI, O_DWI = 1792, 2048, 2176, 2304, 2560, 2688
N_PROJ = 2816

FFN_TM, FFN_TF = 1024, 256
PROJ_TM = 256
ATT_T = 256
MIX_TM = 512


def _dot(a, b):
    return jnp.dot(a, b, preferred_element_type=F32)


def _dot_nt(a, b):
    return lax.dot_general(a, b, (((1,), (1,)), ((), ())), preferred_element_type=F32)


def _rmsnorm(x, g):
    return x * lax.rsqrt(jnp.mean(x * x, axis=-1, keepdims=True) + NORM_EPS) * g


def _vec_spec(width, layer):
    return pl.BlockSpec((None, 1, width), lambda *_: (layer, 0, 0))


def _params(*sem):
    return pltpu.CompilerParams(dimension_semantics=sem, vmem_limit_bytes=VMEM_LIMIT)


def _ffn_kernel(x_ref, g_ref, wg_ref, wu_ref, wo_ref, fg_ref, o_ref, h_ref, acc_ref, *, final_norm):
    j = pl.program_id(1)

    @pl.when(j == 0)
    def _():
        h_ref[...] = _rmsnorm(x_ref[...], g_ref[...]).astype(BF16)
        acc_ref[...] = jnp.zeros_like(acc_ref)

    h = h_ref[...]
    gate = _dot(h, wg_ref[...])
    up = _dot(h, wu_ref[...])
    act = (gate * jax.nn.sigmoid(gate) * up).astype(BF16)
    acc_ref[...] += _dot(act, wo_ref[...])

    @pl.when(j == pl.num_programs(1) - 1)
    def _():
        y = x_ref[...] + 0.5 * acc_ref[...]
        if final_norm:
            y = _rmsnorm(y, fg_ref[...])
        o_ref[...] = y


def _ffn(x2d, gain, wi, wo, layer, final_gain=None):
    m, d = x2d.shape
    f = wo.shape[1]
    nf = f // FFN_TF
    fg = gain if final_gain is None else final_gain
    return pl.pallas_call(
        functools.partial(_ffn_kernel, final_norm=final_gain is not None),
        grid=(m // FFN_TM, nf),
        in_specs=[
            pl.BlockSpec((FFN_TM, d), lambda i, j: (i, 0)),
            _vec_spec(d, layer),
            pl.BlockSpec((None, d, FFN_TF), lambda i, j: (layer, 0, j)),
            pl.BlockSpec((None, d, FFN_TF), lambda i, j: (layer, 0, j + nf)),
            pl.BlockSpec((None, FFN_TF, d), lambda i, j: (layer, j, 0)),
            _vec_spec(d, 0 if final_gain is not None else layer),
        ],
        out_specs=pl.BlockSpec((FFN_TM, d), lambda i, j: (i, 0)),
        out_shape=jax.ShapeDtypeStruct((m, d), F32),
        scratch_shapes=[pltpu.VMEM((FFN_TM, d), BF16), pltpu.VMEM((FFN_TM, d), F32)],
        compiler_params=_params("parallel", "arbitrary"),
        name="ffn",
    )(x2d, gain, wi, wi, wo, fg)


def _rope128(x, cos, sin, half):
    lane = lax.broadcasted_iota(jnp.int32, x.shape, 1)
    first = (lane % (2 * half)) < half
    partner = jnp.where(first, pltpu.roll(x, LANES - half, 1), pltpu.roll(x, half, 1))
    return x * cos + partner * sin


def _proj_kernel(x_ref, g_ref, win_ref, bf_ref, gq_ref, wuq_ref, gkv_ref, wukv_ref,
                 c32_ref, s32_ref, c64_ref, s64_ref,
                 fq_ref, fk_ref, fv_ref, cq_ref, ck_ref,
                 mq_ref, mk_ref, mv_ref,
                 dq_ref, dk_ref, dv_ref, qi_ref, ki_ref, wi_ref,
                 carry_ref):
    j = pl.program_id(1)
    tm = x_ref.shape[1]

    @pl.when(j == 0)
    def _():
        carry_ref[...] = jnp.zeros_like(carry_ref)

    h = _rmsnorm(x_ref[0], g_ref[...]).astype(BF16)
    p = _dot(h, win_ref[...])

    def grp(off, width=LANES):
        return p[:, off:off + width]

    c32, s32 = c32_ref[...], s32_ref[...]
    c64, s64 = c64_ref[...], s64_ref[...]

    fq_ref[0] = (grp(O_FQ, 384) * (FOX_DIM ** -0.5)).astype(BF16)
    fk_ref[0] = grp(O_FK, 384).astype(BF16)
    fv_ref[0, 0] = grp(O_FV, 384).T.astype(BF16)

    z = grp(O_FF) + bf_ref[...]
    logf = jnp.minimum(z, 0.0) - jnp.log(1.0 + jnp.exp(-jnp.abs(z)))
    hi = logf.astype(BF16)
    r1 = logf - hi.astype(F32)
    mid = r1.astype(BF16)
    lo = (r1 - mid.astype(F32)).astype(BF16)
    row = lax.broadcasted_iota(jnp.int32, (tm, tm), 0)
    col = lax.broadcasted_iota(jnp.int32, (tm, tm), 1)
    tri = jnp.where(row >= col, 1.0, 0.0).astype(BF16)
    cum = (_dot(tri, hi) + _dot(tri, mid)) + _dot(tri, lo) + carry_ref[0:1, :]
    carry_ref[0:1, :] = cum[tm - 1:tm, :]
    cq_ref[0] = cum
    ck_ref[0, 0] = cum.T[0:8, :]

    qn = _rmsnorm(grp(O_MCQ, MLA_Q_RANK), gq_ref[...]).astype(BF16)
    qm = _dot(qn, wuq_ref[...]) * ((MLA_NOPE + MLA_ROPE) ** -0.5)
    kvn = _rmsnorm(grp(O_MCKV, MLA_KV_RANK), gkv_ref[...]).astype(BF16)
    kvm = _dot(kvn, wukv_ref[...])
    krt = _rope128(grp(O_MKR), c32, s32, MLA_ROPE // 2).astype(BF16)
    for pr in range(MLA_HEADS // 2):
        mq_ref[0, :, 256 * pr:256 * pr + 128] = qm[:, 256 * pr:256 * pr + 128].astype(BF16)
        mq_ref[0, :, 256 * pr + 128:256 * pr + 256] = _rope128(
            qm[:, 256 * pr + 128:256 * pr + 256], c32, s32, MLA_ROPE // 2).astype(BF16)
        mk_ref[0, :, 256 * pr:256 * pr + 128] = kvm[:, 128 * pr:128 * pr + 128].astype(BF16)
        mk_ref[0, :, 256 * pr + 128:256 * pr + 256] = krt
    mv_ref[0, 0] = kvm[:, 384:768].T.astype(BF16)

    for gi in range(2):
        dq_ref[0, :, 128 * gi:128 * gi + 128] = (
            _rope128(grp(O_DQ + 128 * gi), c64, s64, DSA_DIM // 2) * (DSA_DIM ** -0.5)).astype(BF16)
        qi_ref[0, :, 128 * gi:128 * gi + 128] = (
            _rope128(grp(O_DQI + 128 * gi), c32, s32, IDX_DIM // 2) * (IDX_DIM ** -0.5)).astype(BF16)
    dk_ref[0] = _rope128(grp(O_DK), c64, s64, DSA_DIM // 2).astype(BF16)
    dv_ref[0, 0] = grp(O_DV).T[0:DSA_DIM, :].astype(BF16)
    ki_ref[0] = _rope128(grp(O_DKI), c32, s32, IDX_DIM // 2).astype(BF16)
    wi_ref[0, 0] = (grp(O_DWI) * (IDX_HEADS ** -0.5)).T[0:IDX_HEADS, :]


def _proj(x, gain, win, bfp, gq, wuq, gkv, wukv, tabs, layer):
    b, s, d = x.shape
    tm = PROJ_TM
    nt = s // tm
    tok = lambda w, dt: jax.ShapeDtypeStruct((b, s, w), dt)
    tspec = lambda w: pl.BlockSpec((1, tm, w), lambda bi, j: (bi, j, 0))
    full = lambda shape: pl.BlockSpec((None,) + shape, lambda bi, j: (layer,) + (0,) * len(shape))
    vec = lambda w: _vec_spec(w, layer)
    tab = pl.BlockSpec((tm, LANES), lambda bi, j: (j, 0))
    tr = lambda r, dt: jax.ShapeDtypeStruct((b, nt, r, tm), dt)
    trspec = lambda r: pl.BlockSpec((1, 1, r, tm), lambda bi, j: (bi, j, 0, 0))
    out_shape = [tok(384, BF16), tok(384, BF16), tr(384, BF16), tok(LANES, F32), tr(8, F32),
                 tok(768, BF16), tok(768, BF16), tr(384, BF16),
                 tok(256, BF16), tok(LANES, BF16), tr(DSA_DIM, BF16),
                 tok(256, BF16), tok(LANES, BF16), tr(IDX_HEADS, F32)]
    out_specs = [tspec(384), tspec(384), trspec(384), tspec(LANES), trspec(8),
                 tspec(768), tspec(768), trspec(384),
                 tspec(256), tspec(LANES), trspec(DSA_DIM),
                 tspec(256), tspec(LANES), trspec(IDX_HEADS)]
    return pl.pallas_call(
        _proj_kernel,
        grid=(b, nt),
        in_specs=[tspec(d), vec(d), full((d, N_PROJ)), vec(LANES),
                  vec(MLA_Q_RANK), full((MLA_Q_RANK, 768)), vec(MLA_KV_RANK), full((MLA_KV_RANK, 768)),
                  tab, tab, tab, tab],
        out_specs=out_specs,
        out_shape=out_shape,
        scratch_shapes=[pltpu.VMEM((8, LANES), F32)],
        compiler_params=_params("arbitrary", "arbitrary"),
        name="proj",
    )(x, gain, win, bfp, gq, wuq, gkv, wukv, *tabs)


def _softmax_steps(sts, carry, vts):
    stats = []
    for st, (m, l, _) in zip(sts, carry):
        m_new = jnp.maximum(m, jnp.max(st, axis=0, keepdims=True))
        alpha = jnp.exp(m - m_new)
        pt = jnp.exp(st - m_new)
        stats.append((m_new, alpha, alpha * l + jnp.sum(pt, axis=0, keepdims=True), pt.astype(BF16)))
    pvs = [_dot(vt, pt) for vt, (_, _, _, pt) in zip(vts, stats)]
    return tuple((m_new, l, alpha * acc + pv)
                 for (m_new, alpha, l, _), (_, _, acc), pv in zip(stats, carry, pvs))


def _softmax_init(t, dv):
    return (jnp.full((1, t), NEG, F32), jnp.zeros((1, t), F32), jnp.zeros((dv, t), F32))


def _pair_output(carry):
    ot = jnp.concatenate([carry[0][2] / carry[0][1], carry[1][2] / carry[1][1]], axis=0)
    return ot.T.astype(BF16)


def _attn_kernel(*refs, n_pairs, kw, has_bias):
    if has_bias:
        q_ref, k_ref, vt_ref, ccol_ref, crow_ref, o_ref = refs
    else:
        q_ref, k_ref, vt_ref, o_ref = refs
    i = pl.program_id(1)
    t = q_ref.shape[1]
    lane = lax.broadcasted_iota(jnp.int32, (t, kw), 1)
    kpos = lax.broadcasted_iota(jnp.int32, (t, t), 0)
    qpos = lax.broadcasted_iota(jnp.int32, (t, t), 1)
    causal = kpos <= qpos

    qh = []
    for pr in range(n_pairs):
        qs = q_ref[0, :, kw * pr:kw * (pr + 1)]
        for e in range(2):
            sel = (lane >= 64 * e) & (lane < 64 * e + 64)
            if kw == 256:
                sel = sel | ((lane >= 128 + 32 * e) & (lane < 160 + 32 * e))
            qh.append(jnp.where(sel, qs, jnp.zeros_like(qs)))

    def chunk(c, carry, diag):
        start = pl.multiple_of(c * t, t)
        heads = range(2 * n_pairs)
        sts = [_dot_nt(k_ref[0, pl.ds(start, t), kw * (hd // 2):kw * (hd // 2 + 1)], qh[hd]) for hd in heads]
        if has_bias:
            sts = [st + (crow_ref[0, 0, hd:hd + 1, :] - ccol_ref[0, pl.ds(start, t), hd:hd + 1])
                   for hd, st in zip(heads, sts)]
        if diag:
            sts = [jnp.where(causal, st, NEG) for st in sts]
        return _softmax_steps(sts, carry, [vt_ref[0, c, 64 * hd:64 * (hd + 1), :] for hd in heads])

    init = tuple(_softmax_init(t, 64) for _ in range(2 * n_pairs))
    carry = lax.fori_loop(0, i, lambda c, cr: chunk(c, cr, False), init)
    carry = chunk(i, carry, True)
    for pr in range(n_pairs):
        o_ref[0, :, LANES * pr:LANES * (pr + 1)] = _pair_output(carry[2 * pr:2 * pr + 2])


def _attn(q, k, vt, ccol=None, crow=None, *, kw, name):
    b, s, qw = q.shape
    n_pairs = qw // kw
    t = ATT_T
    has_bias = ccol is not None
    in_specs = [pl.BlockSpec((1, t, qw), lambda bi, i: (bi, i, 0)),
                pl.BlockSpec((1, s, qw), lambda bi, i: (bi, 0, 0)),
                pl.BlockSpec((1, s // t, n_pairs * LANES, t), lambda bi, i: (bi, 0, 0, 0))]
    args = [q, k, vt]
    if has_bias:
        in_specs += [pl.BlockSpec((1, s, LANES), lambda bi, i: (bi, 0, 0)),
                     pl.BlockSpec((1, 1, 8, t), lambda bi, i: (bi, i, 0, 0))]
        args += [ccol, crow]
    return pl.pallas_call(
        functools.partial(_attn_kernel, n_pairs=n_pairs, kw=kw, has_bias=has_bias),
        grid=(b, s // t),
        in_specs=in_specs,
        out_specs=pl.BlockSpec((1, t, n_pairs * LANES), lambda bi, i: (bi, i, 0)),
        out_shape=jax.ShapeDtypeStruct((b, s, n_pairs * LANES), BF16),
        compiler_params=_params("parallel", "arbitrary"),
        name=name,
    )(*args)


def _sortable(x):
    bits = lax.bitcast_convert_type(x + 0.0, jnp.int32)
    return jnp.where(bits < 0, bits ^ 0x7FFFFFFF, bits)


def _dsa_kernel(q_ref, k_ref, vt_ref, qi_ref, ki_ref, wit_ref, o_ref, key_ref, bias_ref, *, n_sel):
    i = pl.program_id(1)
    t = q_ref.shape[1]
    nc = i + 1
    lane = lax.broadcasted_iota(jnp.int32, (t, LANES), 1)
    kpos = lax.broadcasted_iota(jnp.int32, (t, t), 0)
    qpos = lax.broadcasted_iota(jnp.int32, (t, t), 1)

    def visible(c):
        return (kpos - qpos) <= (i - c) * t

    qi = qi_ref[0]
    qih, wih = [], []
    for hd in range(IDX_HEADS):
        g, e = divmod(hd, LANES // IDX_DIM)
        grp = qi[:, LANES * g:LANES * (g + 1)]
        qih.append(jnp.where((lane >= IDX_DIM * e) & (lane < IDX_DIM * (e + 1)), grp, jnp.zeros_like(grp)))
        wih.append(wit_ref[0, 0, hd:hd + 1, :])

    def score_chunk(c, _):
        start = pl.multiple_of(c * t, t)
        kic = ki_ref[0, pl.ds(start, t), :]
        isc = jnp.zeros((t, t), F32)
        for hd in range(IDX_HEADS):
            isc = isc + jnp.maximum(_dot_nt(kic, qih[hd]), 0.0) * wih[hd]
        key_ref[c] = jnp.where(visible(c), _sortable(isc), INT_MIN)
        return 0

    lax.fori_loop(0, nc, score_chunk, 0)

    def count(pred):
        def body(c, cnt):
            hit = pred(key_ref[c], kpos + c * t)
            return cnt + jnp.sum(jnp.where(hit, 1.0, 0.0), axis=0, keepdims=True)
        return lax.fori_loop(0, nc, body, jnp.zeros((1, t), F32))

    ge0 = count(lambda kc, idx: kc >= 0)
    thr = jnp.where(ge0 >= n_sel, 0, INT_MIN).astype(jnp.int32)

    def thr_step(b, thr):
        cand = thr | lax.shift_left(jnp.int32(1), 30 - b)
        cnt = count(lambda kc, idx: kc >= cand)
        return jnp.where(cnt >= n_sel, cand, thr)

    thr = lax.fori_loop(0, 31, thr_step, thr)

    need = n_sel - count(lambda kc, idx: kc > thr)

    def cut_step(b, cut):
        cand = cut | lax.shift_left(jnp.int32(1), 10 - b)
        cnt = count(lambda kc, idx: (kc == thr) & (idx < cand))
        return jnp.where(cnt < need, cand, cut)

    nbits = max(1, int(np.ceil(np.log2(k_ref.shape[1]))))
    cut = lax.fori_loop(11 - nbits, 11, cut_step, jnp.zeros((1, t), jnp.int32))

    def bias_chunk(c, _):
        kc = key_ref[c]
        keep = visible(c) & ((kc > thr) | ((kc == thr) & (kpos + c * t <= cut)))
        bias_ref[c] = jnp.where(keep, 0.0, NEG)
        return 0

    lax.fori_loop(0, nc, bias_chunk, 0)

    q = q_ref[0]
    qh = []
    for hd in range(DSA_HEADS):
        g, e = divmod(hd, 2)
        grp = q[:, LANES * g:LANES * (g + 1)]
        qh.append(jnp.where((lane >= 64 * e) & (lane < 64 * (e + 1)), grp, jnp.zeros_like(grp)))

    def att_chunk(c, carry):
        start = pl.multiple_of(c * t, t)
        ks = k_ref[0, pl.ds(start, t), :]
        sts = [_dot_nt(ks, qh[hd]) for hd in range(DSA_HEADS)]
        sts = [st + bias_ref[c] for st in sts]
        return _softmax_steps(sts, carry, [vt_ref[0, c]] * DSA_HEADS)

    carry = lax.fori_loop(0, nc, att_chunk, tuple(_softmax_init(t, DSA_DIM) for _ in range(DSA_HEADS)))
    for g in range(DSA_HEADS // 2):
        o_ref[0, :, LANES * g:LANES * (g + 1)] = _pair_output(carry[2 * g:2 * g + 2])


def _dsa(q, k, vt, qi, ki, wit, n_sel):
    b, s, _ = q.shape
    t = ATT_T
    tile = lambda w: pl.BlockSpec((1, t, w), lambda bi, i: (bi, i, 0))
    seq = lambda w: pl.BlockSpec((1, s, w), lambda bi, i: (bi, 0, 0))
    return pl.pallas_call(
        functools.partial(_dsa_kernel, n_sel=n_sel),
        grid=(b, s // t),
        in_specs=[tile(256), seq(LANES), pl.BlockSpec((1, s // t, DSA_DIM, t), lambda bi, i: (bi, 0, 0, 0)),
                  tile(256), seq(LANES), pl.BlockSpec((1, 1, IDX_HEADS, t), lambda bi, i: (bi, i, 0, 0))],
        out_specs=tile(256),
        out_shape=jax.ShapeDtypeStruct((b, s, 256), BF16),
        scratch_shapes=[pltpu.VMEM((s // t, t, t), jnp.int32), pltpu.VMEM((s // t, t, t), F32)],
        compiler_params=_params("parallel", "arbitrary"),
        name="dsa",
    )(q, k, vt, qi, ki, wit)


def _memkv_kernel(m_ref, g_ref, w_ref, k_ref, v_ref):
    hm = _rmsnorm(m_ref[0], g_ref[...]).astype(BF16)
    kv = _dot(hm, w_ref[...])
    half = kv.shape[1] // 2
    k_ref[0] = kv[:, :half].astype(BF16)
    v_ref[0] = kv[:, half:].astype(BF16)


def _memkv(mem, gain, wkv, layer):
    b, ml, d = mem.shape
    n = XA_HEADS * XA_DIM
    return pl.pallas_call(
        _memkv_kernel,
        grid=(b,),
        in_specs=[pl.BlockSpec((1, ml, d), lambda bi: (bi, 0, 0)),
                  _vec_spec(d, layer),
                  pl.BlockSpec((None, d, 2 * n), lambda bi: (layer, 0, 0))],
        out_specs=[pl.BlockSpec((1, ml, n), lambda bi: (bi, 0, 0))] * 2,
        out_shape=[jax.ShapeDtypeStruct((b, ml, n), BF16)] * 2,
        compiler_params=_params("parallel"),
        name="memkv",
    )(mem, gain, wkv)


def _mix_kernel(x_ref, a_ref, b_ref, c_ref, wout_ref, g_ref, wq_ref, kx_ref, vx_ref, wo_ref, o_ref):
    mixed = jnp.concatenate([a_ref[0], b_ref[0], c_ref[0]], axis=-1)
    x1 = x_ref[0] + _dot(mixed, wout_ref[...])
    hq = _rmsnorm(x1, g_ref[...]).astype(BF16)
    q = (_dot(hq, wq_ref[...]) * (XA_DIM ** -0.5)).astype(BF16)
    outs = []
    for hd in range(XA_HEADS):
        sl = slice(XA_DIM * hd, XA_DIM * (hd + 1))
        s = _dot_nt(q[:, sl], kx_ref[0, :, sl])
        pe = jnp.exp(s - jnp.max(s, axis=-1, keepdims=True))
        o = _dot(pe.astype(BF16), vx_ref[0, :, sl]) / jnp.sum(pe, axis=-1, keepdims=True)
        outs.append(o.astype(BF16))
    o_ref[0] = x1 + _dot(jnp.concatenate(outs, axis=-1), wo_ref[...])


def _mix_xattn(x, a, bm, c, wout, gain, wq, kx, vx, wo, layer):
    b, s, d = x.shape
    tm = MIX_TM
    n = XA_HEADS * XA_DIM
    ml = kx.shape[1]
    tile = lambda w: pl.BlockSpec((1, tm, w), lambda bi, j: (bi, j, 0))
    full = lambda r, cdim: pl.BlockSpec((None, r, cdim), lambda bi, j: (layer, 0, 0))
    mem = pl.BlockSpec((1, ml, n), lambda bi, j: (bi, 0, 0))
    return pl.pallas_call(
        _mix_kernel,
        grid=(b, s // tm),
        in_specs=[tile(d), tile(a.shape[2]), tile(bm.shape[2]), tile(c.shape[2]), full(d, d),
                  _vec_spec(d, layer), full(d, n), mem, mem, full(n, d)],
        out_specs=tile(d),
        out_shape=jax.ShapeDtypeStruct((b, s, d), F32),
        compiler_params=_params("parallel", "parallel"),
        name="mix_xattn",
    )(x, a, bm, c, wout, gain, wq, kx, vx, wo)


def _rope_tables(s, d):
    half = d // 2
    inv = ROPE_THETA ** (-jnp.arange(half, dtype=F32) / half)
    ang = jnp.arange(s).astype(F32)[:, None] * inv[None, :]
    cos, sin = jnp.cos(ang), jnp.sin(ang)
    reps = LANES // d
    return (jnp.tile(jnp.concatenate([cos, cos], -1), (1, reps)),
            jnp.tile(jnp.concatenate([-sin, sin], -1), (1, reps)))


def _layout_w_in(w_in):
    sizes = (384, 384, 384, FOX_HEADS, MLA_Q_RANK, MLA_KV_RANK, MLA_ROPE,
             DSA_HEADS * DSA_DIM, DSA_DIM, DSA_DIM, IDX_HEADS * IDX_DIM, IDX_DIM, IDX_HEADS)
    offs = np.concatenate([[0], np.cumsum(sizes)])
    fq, fk, fv, ff, mcq, mckv, mkr, dq, dk, dv, dqi, dki, dwi = (
        w_in[:, :, offs[n]:offs[n + 1]] for n in range(len(sizes)))
    zeros = lambda w: jnp.zeros(w_in.shape[:2] + (w,), w_in.dtype)
    pad = lambda a: jnp.concatenate([a, zeros(LANES - a.shape[2])], -1)
    out = jnp.concatenate([
        fq, fk, fv, pad(ff), mcq, mckv,
        jnp.concatenate([mkr, mkr, zeros(64)], -1),
        dq, jnp.concatenate([dk, dk], -1), jnp.concatenate([dv, dv], -1),
        dqi, jnp.concatenate([dki] * 4, -1), pad(dwi)], -1)
    assert out.shape[2] == N_PROJ
    return out.astype(BF16)


def _layout_w_uq(w_uq):
    hw = MLA_NOPE + MLA_ROPE
    cols = []
    for pr in range(MLA_HEADS // 2):
        h0, h1 = 2 * pr, 2 * pr + 1
        cols += [w_uq[:, :, hw * h0:hw * h0 + MLA_NOPE], w_uq[:, :, hw * h1:hw * h1 + MLA_NOPE],
                 w_uq[:, :, hw * h0 + MLA_NOPE:hw * (h0 + 1)], w_uq[:, :, hw * h1 + MLA_NOPE:hw * (h1 + 1)],
                 jnp.zeros(w_uq.shape[:2] + (64,), w_uq.dtype)]
    return jnp.concatenate(cols, -1).astype(BF16)


def _layout_w_ukv(w_ukv):
    hw = MLA_NOPE + MLA_V
    kn = [w_ukv[:, :, hw * hd:hw * hd + MLA_NOPE] for hd in range(MLA_HEADS)]
    vv = [w_ukv[:, :, hw * hd + MLA_NOPE:hw * (hd + 1)] for hd in range(MLA_HEADS)]
    return jnp.concatenate(kn + vv, -1).astype(BF16)


def kernel(x, mem, ffn1_norm, ffn1_wi, ffn1_wo, mix_norm, w_in, b_forget, mla_q_norm, mla_w_uq,
           mla_kv_norm, mla_w_ukv, w_out, xa_norm, mem_norm, xa_wq, xa_wkv, xa_wo,
           ffn2_norm, ffn2_wi, ffn2_wo, final_norm):
    b, s, d = x.shape
    depth = w_in.shape[0]
    assert s % ATT_T == 0 and s % MIX_TM == 0 and (b * s) % FFN_TM == 0 and s <= 2048
    n_sel = min(TOPK_MAX, s // 4)
    assert PROJ_TM == ATT_T and n_sel <= ATT_T

    win = _layout_w_in(w_in)
    wuq = _layout_w_uq(mla_w_uq)
    wukv = _layout_w_ukv(mla_w_ukv)
    rows = lambda v: v.reshape(-1, 1, v.shape[-1])
    bfp = rows(jnp.concatenate([b_forget, jnp.zeros((depth, LANES - FOX_HEADS), F32)], -1))
    ffn1_norm, mix_norm, mla_q_norm, mla_kv_norm, xa_norm, mem_norm, ffn2_norm, final_norm = map(
        rows, (ffn1_norm, mix_norm, mla_q_norm, mla_kv_norm, xa_norm, mem_norm, ffn2_norm, final_norm))
    tabs = _rope_tables(s, IDX_DIM) + _rope_tables(s, DSA_DIM)
    w1i, w1o = ffn1_wi.astype(BF16), ffn1_wo.astype(BF16)
    w2i, w2o = ffn2_wi.astype(BF16), ffn2_wo.astype(BF16)
    wout, wq, wkv, wo = (w.astype(BF16) for w in (w_out, xa_wq, xa_wkv, xa_wo))

    for layer in range(depth):
        x = _ffn(x.reshape(b * s, d), ffn1_norm, w1i, w1o, layer).reshape(b, s, d)
        (fq, fk, fv, cq, ck, mq, mk, mv, dq, dk, dv, qi, ki, wi) = _proj(
            x, mix_norm, win, bfp, mla_q_norm, wuq, mla_kv_norm, wukv, tabs, layer)
        a_out = _attn(fq, fk, fv, cq, ck, kw=LANES, name="fox_attn")
        b_out = _attn(mq, mk, mv, kw=2 * LANES, name="mla_attn")
        c_out = _dsa(dq, dk, dv, qi, ki, wi, n_sel)
        kx, vx = _memkv(mem, mem_norm, wkv, layer)
        x = _mix_xattn(x, a_out, b_out, c_out, wout, xa_norm, wq, kx, vx, wo, layer)
        last = layer == depth - 1
        x = _ffn(x.reshape(b * s, d), ffn2_norm, w2i, w2o, layer,
                 final_gain=final_norm if last else None).reshape(b, s, d)
    return x
```

```python
import functools

import jax
import jax.numpy as jnp
import numpy as np
from jax import lax
from jax.experimental import pallas as pl
from jax.experimental.pallas import tpu as pltpu

F32 = jnp.float32
BF16 = jnp.bfloat16

ROPE_THETA = 10000.0
NORM_EPS = 1e-6
FOX_HEADS, FOX_DIM = 6, 64
MLA_HEADS, MLA_NOPE, MLA_ROPE, MLA_V = 6, 64, 32, 64
MLA_Q_RANK, MLA_KV_RANK = 256, 128
DSA_HEADS, DSA_DIM = 4, 64
IDX_HEADS, IDX_DIM = 8, 32
TOPK_MAX = 256
XA_HEADS, XA_DIM = 4, 128

LANES = 128
VMEM_LIMIT = 56 * 1024 * 1024
NEG = -1e30
LOG2E = 1.4426950408889634
INT_MIN = -(2 ** 31)
DIGIT_BITS = (14, 14, 4)
DIGIT_BIAS = 0x80

O_FQ, O_FK, O_FV, O_FF = 0, 384, 768, 1152
O_MCQ, O_MCKV, O_MKR = 1280, 1536, 1664
O_DQ, O_DK, O_DV, O_DQI, O_DKI, O_DWI = 1792, 2048, 2176, 2304, 2560, 2688
N_PROJ = 2816

FFN_TM, FFN_TF = 1024, 256
PROJ_TM = 256
ATT_T = 256
MIX_TM = 512


def _dot(a, b):
    return jnp.dot(a, b, preferred_element_type=F32)


def _dot_nt(a, b):
    return lax.dot_general(a, b, (((1,), (1,)), ((), ())), preferred_element_type=F32)


def _rmsnorm(x, g):
    return x * lax.rsqrt(jnp.mean(x * x, axis=-1, keepdims=True) + NORM_EPS) * g


def _vec_spec(width, layer):
    return pl.BlockSpec((None, 1, width), lambda *_: (layer, 0, 0))


def _params(*sem):
    return pltpu.CompilerParams(dimension_semantics=sem, vmem_limit_bytes=VMEM_LIMIT)


def _ffn_kernel(x_ref, g_ref, wi_ref, wo_ref, fg_ref, o_ref, h_ref, act_ref, *, final_norm):
    f = wo_ref.shape[0]
    h_ref[...] = _rmsnorm(x_ref[...], g_ref[...]).astype(BF16)
    for j in range(f // FFN_TF):
        lo, hi = j * FFN_TF, (j + 1) * FFN_TF
        gate = _dot(h_ref[...], wi_ref[:, lo:hi])
        up = _dot(h_ref[...], wi_ref[:, f + lo:f + hi])
        act_ref[:, lo:hi] = (gate * jax.nn.sigmoid(gate) * up).astype(BF16)
    y = x_ref[...] + 0.5 * _dot(act_ref[...], wo_ref[...])
    if final_norm:
        y = _rmsnorm(y, fg_ref[...])
    o_ref[...] = y


def _ffn(x2d, gain, wi, wo, layer, final_gain=None):
    m, d = x2d.shape
    f = wo.shape[1]
    assert f % FFN_TF == 0
    fg = gain if final_gain is None else final_gain
    resident = lambda r, c: pl.BlockSpec((None, r, c), lambda i: (layer, 0, 0), pipeline_mode=pl.Buffered(1))
    return pl.pallas_call(
        functools.partial(_ffn_kernel, final_norm=final_gain is not None),
        grid=(m // FFN_TM,),
        in_specs=[
            pl.BlockSpec((FFN_TM, d), lambda i: (i, 0)),
            _vec_spec(d, layer),
            resident(d, 2 * f),
            resident(f, d),
            _vec_spec(d, 0 if final_gain is not None else layer),
        ],
        out_specs=pl.BlockSpec((FFN_TM, d), lambda i: (i, 0)),
        out_shape=jax.ShapeDtypeStruct((m, d), F32),
        scratch_shapes=[pltpu.VMEM((FFN_TM, d), BF16), pltpu.VMEM((FFN_TM, f), BF16)],
        compiler_params=_params("parallel"),
        name="ffn",
    )(x2d, gain, wi, wo, fg)


def _rope128(x, cos, sin, half):
    lane = lax.broadcasted_iota(jnp.int32, x.shape, 1)
    first = (lane % (2 * half)) < half
    partner = jnp.where(first, pltpu.roll(x, LANES - half, 1), pltpu.roll(x, half, 1))
    return x * cos + partner * sin


def _proj_kernel(x_ref, g_ref, win_ref, bf_ref, gq_ref, wuq_ref, gkv_ref, wukv_ref,
                 c32_ref, s32_ref, c64_ref, s64_ref,
                 fq_ref, fk_ref, fv_ref, cq_ref, ck_ref,
                 mq_ref, mk_ref, mv_ref,
                 dq_ref, dk_ref, dv_ref, qi_ref, ki_ref, wi_ref,
                 carry_ref):
    j = pl.program_id(1)
    tm = x_ref.shape[1]

    @pl.when(j == 0)
    def _():
        carry_ref[...] = jnp.zeros_like(carry_ref)

    h = _rmsnorm(x_ref[0], g_ref[...]).astype(BF16)
    p = _dot(h, win_ref[...])

    def grp(off, width=LANES):
        return p[:, off:off + width]

    c32, s32 = c32_ref[...], s32_ref[...]
    c64, s64 = c64_ref[...], s64_ref[...]

    fq_ref[0] = (grp(O_FQ, 384) * (FOX_DIM ** -0.5 * LOG2E)).astype(BF16)
    fk_ref[0] = grp(O_FK, 384).astype(BF16)
    fv_ref[0, 0] = grp(O_FV, 384).T.astype(BF16)

    z = grp(O_FF) + bf_ref[...]
    logf = jnp.minimum(z, 0.0) - jnp.log(1.0 + jnp.exp(-jnp.abs(z)))
    hi = logf.astype(BF16)
    r1 = logf - hi.astype(F32)
    mid = r1.astype(BF16)
    lo = (r1 - mid.astype(F32)).astype(BF16)
    row = lax.broadcasted_iota(jnp.int32, (tm, tm), 0)
    col = lax.broadcasted_iota(jnp.int32, (tm, tm), 1)
    tri = jnp.where(row >= col, 1.0, 0.0).astype(BF16)
    cum = (_dot(tri, hi) + _dot(tri, mid)) + _dot(tri, lo) + carry_ref[0:1, :]
    carry_ref[0:1, :] = cum[tm - 1:tm, :]
    cum2 = cum * LOG2E
    cq_ref[0] = cum2
    ck_ref[0, 0] = cum2.T[0:8, :]

    qn = _rmsnorm(grp(O_MCQ, MLA_Q_RANK), gq_ref[...]).astype(BF16)
    qm = _dot(qn, wuq_ref[...]) * ((MLA_NOPE + MLA_ROPE) ** -0.5 * LOG2E)
    kvn = _rmsnorm(grp(O_MCKV, MLA_KV_RANK), gkv_ref[...]).astype(BF16)
    kvm = _dot(kvn, wukv_ref[...])
    krt = _rope128(grp(O_MKR), c32, s32, MLA_ROPE // 2).astype(BF16)
    for pr in range(MLA_HEADS // 2):
        mq_ref[0, :, 256 * pr:256 * pr + 128] = qm[:, 256 * pr:256 * pr + 128].astype(BF16)
        mq_ref[0, :, 256 * pr + 128:256 * pr + 256] = _rope128(
            qm[:, 256 * pr + 128:256 * pr + 256], c32, s32, MLA_ROPE // 2).astype(BF16)
        mk_ref[0, :, 256 * pr:256 * pr + 128] = kvm[:, 128 * pr:128 * pr + 128].astype(BF16)
        mk_ref[0, :, 256 * pr + 128:256 * pr + 256] = krt
    mv_ref[0, 0] = kvm[:, 384:768].T.astype(BF16)

    for gi in range(2):
        dq_ref[0, :, 128 * gi:128 * gi + 128] = (
            _rope128(grp(O_DQ + 128 * gi), c64, s64, DSA_DIM // 2) * (DSA_DIM ** -0.5 * LOG2E)).astype(BF16)
        qi_ref[0, :, 128 * gi:128 * gi + 128] = (
            _rope128(grp(O_DQI + 128 * gi), c32, s32, IDX_DIM // 2) * (IDX_DIM ** -0.5)).astype(BF16)
    dk_ref[0] = _rope128(grp(O_DK), c64, s64, DSA_DIM // 2).astype(BF16)
    dv_ref[0, 0] = grp(O_DV).T[0:DSA_DIM, :].astype(BF16)
    ki_ref[0] = _rope128(grp(O_DKI), c32, s32, IDX_DIM // 2).astype(BF16)
    wi_ref[0, 0] = (grp(O_DWI) * (IDX_HEADS ** -0.5)).T[0:IDX_HEADS, :]


def _proj(x, gain, win, bfp, gq, wuq, gkv, wukv, tabs, layer):
    b, s, d = x.shape
    tm = PROJ_TM
    nt = s // tm
    tok = lambda w, dt: jax.ShapeDtypeStruct((b, s, w), dt)
    tspec = lambda w: pl.BlockSpec((1, tm, w), lambda bi, j: (bi, j, 0))
    full = lambda shape: pl.BlockSpec((None,) + shape, lambda bi, j: (layer,) + (0,) * len(shape))
    vec = lambda w: _vec_spec(w, layer)
    tab = pl.BlockSpec((tm, LANES), lambda bi, j: (j, 0))
    tr = lambda r, dt: jax.ShapeDtypeStruct((b, nt, r, tm), dt)
    trspec = lambda r: pl.BlockSpec((1, 1, r, tm), lambda bi, j: (bi, j, 0, 0))
    out_shape = [tok(384, BF16), tok(384, BF16), tr(384, BF16), tok(LANES, F32), tr(8, F32),
                 tok(768, BF16), tok(768, BF16), tr(384, BF16),
                 tok(256, BF16), tok(LANES, BF16), tr(DSA_DIM, BF16),
                 tok(256, BF16), tok(LANES, BF16), tr(IDX_HEADS, F32)]
    out_specs = [tspec(384), tspec(384), trspec(384), tspec(LANES), trspec(8),
                 tspec(768), tspec(768), trspec(384),
                 tspec(256), tspec(LANES), trspec(DSA_DIM),
                 tspec(256), tspec(LANES), trspec(IDX_HEADS)]
    return pl.pallas_call(
        _proj_kernel,
        grid=(b, nt),
        in_specs=[tspec(d), vec(d), full((d, N_PROJ)), vec(LANES),
                  vec(MLA_Q_RANK), full((MLA_Q_RANK, 768)), vec(MLA_KV_RANK), full((MLA_KV_RANK, 768)),
                  tab, tab, tab, tab],
        out_specs=out_specs,
        out_shape=out_shape,
        scratch_shapes=[pltpu.VMEM((8, LANES), F32)],
        compiler_params=_params("arbitrary", "arbitrary"),
        name="proj",
    )(x, gain, win, bfp, gq, wuq, gkv, wukv, *tabs)


def _softmax_steps(sts, carry, vts):
    stats = []
    for st, (m, l, _) in zip(sts, carry):
        m_new = jnp.maximum(m, jnp.max(st, axis=0, keepdims=True))
        alpha = jnp.exp2(m - m_new)
        pt = jnp.exp2(st - m_new)
        stats.append((m_new, alpha, alpha * l + jnp.sum(pt, axis=0, keepdims=True), pt.astype(BF16)))
    pvs = [_dot(vt, pt) for vt, (_, _, _, pt) in zip(vts, stats)]
    return tuple((m_new, l, alpha * acc + pv)
                 for (m_new, alpha, l, _), (_, _, acc), pv in zip(stats, carry, pvs))


def _softmax_init(t, dv):
    return (jnp.full((1, t), NEG, F32), jnp.zeros((1, t), F32), jnp.zeros((dv, t), F32))


def _pair_output(carry):
    ot = jnp.concatenate([carry[0][2] / carry[0][1], carry[1][2] / carry[1][1]], axis=0)
    return ot.T.astype(BF16)


def _attn_kernel(*refs, n_pairs, kw, has_bias):
    if has_bias:
        q_ref, k_ref, vt_ref, ccol_ref, crow_ref, o_ref = refs
    else:
        q_ref, k_ref, vt_ref, o_ref = refs
    i = pl.program_id(1)
    t = q_ref.shape[1]
    lane = lax.broadcasted_iota(jnp.int32, (t, kw), 1)
    kpos = lax.broadcasted_iota(jnp.int32, (t, t), 0)
    qpos = lax.broadcasted_iota(jnp.int32, (t, t), 1)
    causal = kpos <= qpos

    qh = []
    for pr in range(n_pairs):
        qs = q_ref[0, :, kw * pr:kw * (pr + 1)]
        for e in range(2):
            sel = (lane >= 64 * e) & (lane < 64 * e + 64)
            if kw == 256:
                sel = sel | ((lane >= 128 + 32 * e) & (lane < 160 + 32 * e))
            qh.append(jnp.where(sel, qs, jnp.zeros_like(qs)))

    heads = range(2 * n_pairs)

    def scores(c, diag):
        start = pl.multiple_of(c * t, t)
        sts = [_dot_nt(k_ref[0, pl.ds(start, t), kw * (hd // 2):kw * (hd // 2 + 1)], qh[hd]) for hd in heads]
        if has_bias:
            sts = [st + (crow_ref[0, 0, hd:hd + 1, :] - ccol_ref[0, pl.ds(start, t), hd:hd + 1])
                   for hd, st in zip(heads, sts)]
        if diag:
            sts = [jnp.where(causal, st, NEG) for st in sts]
        return tuple(sts)

    def values(c):
        return [vt_ref[0, c, 64 * hd:64 * (hd + 1), :] for hd in heads]

    init = tuple(_softmax_init(t, 64) for _ in heads)
    stats = lax.fori_loop(0, i, lambda c, st: _softmax_steps(scores(c, False), st, values(c)), init)
    stats = _softmax_steps(scores(i, True), stats, values(i))
    for pr in range(n_pairs):
        o_ref[0, :, LANES * pr:LANES * (pr + 1)] = _pair_output(stats[2 * pr:2 * pr + 2])


def _attn(q, k, vt, ccol=None, crow=None, *, kw, name):
    b, s, qw = q.shape
    n_pairs = qw // kw
    t = ATT_T
    has_bias = ccol is not None
    in_specs = [pl.BlockSpec((1, t, qw), lambda bi, i: (bi, i, 0)),
                pl.BlockSpec((1, s, qw), lambda bi, i: (bi, 0, 0)),
                pl.BlockSpec((1, s // t, n_pairs * LANES, t), lambda bi, i: (bi, 0, 0, 0))]
    args = [q, k, vt]
    if has_bias:
        in_specs += [pl.BlockSpec((1, s, LANES), lambda bi, i: (bi, 0, 0)),
                     pl.BlockSpec((1, 1, 8, t), lambda bi, i: (bi, i, 0, 0))]
        args += [ccol, crow]
    return pl.pallas_call(
        functools.partial(_attn_kernel, n_pairs=n_pairs, kw=kw, has_bias=has_bias),
        grid=(b, s // t),
        in_specs=in_specs,
        out_specs=pl.BlockSpec((1, t, n_pairs * LANES), lambda bi, i: (bi, i, 0)),
        out_shape=jax.ShapeDtypeStruct((b, s, n_pairs * LANES), BF16),
        compiler_params=_params("parallel", "arbitrary"),
        name=name,
    )(*args)


def _sortable(x):
    bits = lax.bitcast_convert_type(x + 0.0, jnp.int32)
    return jnp.where(bits < 0, bits ^ 0x7FFFFFFF, bits)


def _dsa_kernel(q_ref, k_ref, vt_ref, qi_ref, ki_ref, wit_ref, o_ref, key_ref, bias_ref, dig_ref, *, n_sel):
    i = pl.program_id(1)
    t = q_ref.shape[1]
    nc = i + 1
    lane = lax.broadcasted_iota(jnp.int32, (t, LANES), 1)
    kpos = lax.broadcasted_iota(jnp.int32, (t, t), 0)
    qpos = lax.broadcasted_iota(jnp.int32, (t, t), 1)

    def visible(c):
        return (kpos - qpos) <= (i - c) * t

    qi = qi_ref[0]
    qih, wih = [], []
    for hd in range(IDX_HEADS):
        g, e = divmod(hd, LANES // IDX_DIM)
        grp = qi[:, LANES * g:LANES * (g + 1)]
        qih.append(jnp.where((lane >= IDX_DIM * e) & (lane < IDX_DIM * (e + 1)), grp, jnp.zeros_like(grp)))
        wih.append(wit_ref[0, 0, hd:hd + 1, :])

    def score_chunk(c, _):
        start = pl.multiple_of(c * t, t)
        kic = ki_ref[0, pl.ds(start, t), :]
        isc = jnp.zeros((t, t), F32)
        for hd in range(IDX_HEADS):
            isc = isc + jnp.maximum(_dot_nt(kic, qih[hd]), 0.0) * wih[hd]
        key_ref[c] = jnp.where(visible(c), _sortable(isc), INT_MIN)
        return 0

    lax.fori_loop(0, nc, score_chunk, 0)

    def as_pattern(d):
        return lax.bitcast_convert_type(lax.shift_left(d, 16), F32).astype(BF16)

    def build_digits(shift, nbits, prefix_shift, prefix):
        def body(c, _):
            kc = key_ref[c]
            u = kc ^ INT_MIN
            member = kc != INT_MIN
            if prefix is not None:
                member = member & (lax.shift_right_logical(u, prefix_shift) == prefix)
            d = lax.shift_right_logical(u, shift) & ((1 << nbits) - 1)
            dig_ref[c] = as_pattern(jnp.where(member, d + DIGIT_BIAS, 0))
            return 0
        lax.fori_loop(0, nc, body, 0)

    one, zero = jnp.ones((), BF16), jnp.zeros((), BF16)

    def count_digits(pred):
        def body(c, acc):
            w = jnp.where(pred(dig_ref[c]), one, zero)
            for r in range(t // 16):
                acc = acc + w[16 * r:16 * (r + 1)]
            return acc
        acc = lax.fori_loop(0, nc, body, jnp.zeros((16, t), BF16))
        return jnp.sum(acc.astype(F32), axis=0, keepdims=True)

    def search_digit(nbits, n_above):
        def step(b, dg):
            cand = dg | lax.shift_left(jnp.int32(1), nbits - 1 - b)
            cpat = as_pattern(cand + DIGIT_BIAS)
            cnt = n_above + count_digits(lambda x: x >= cpat)
            return jnp.where(cnt >= n_sel, cand, dg)
        dg = lax.fori_loop(0, nbits, step, jnp.zeros((1, t), jnp.int32))
        dpat = as_pattern(dg + DIGIT_BIAS)
        return dg, dpat, n_above + count_digits(lambda x: x > dpat)

    prefix, n_gt, low = None, jnp.zeros((1, t), F32), 32
    for nbits in DIGIT_BITS:
        build_digits(low - nbits, nbits, low, prefix)
        low -= nbits
        dg, dpat, n_gt = search_digit(nbits, n_gt)
        prefix = dg if prefix is None else lax.shift_left(prefix, nbits) | dg
    thr = prefix ^ INT_MIN

    need = n_sel - n_gt
    n_ties = count_digits(lambda x: x == dpat)

    def tie_cut():
        def count(pred):
            def body(c, cnt):
                hit = pred(key_ref[c], kpos + c * t)
                return cnt + jnp.sum(jnp.where(hit, 1.0, 0.0), axis=0, keepdims=True)
            return lax.fori_loop(0, nc, body, jnp.zeros((1, t), F32))

        def cut_step(b, cut):
            cand = cut | lax.shift_left(jnp.int32(1), 10 - b)
            cnt = count(lambda kc, idx: (kc == thr) & (idx < cand))
            return jnp.where(cnt < need, cand, cut)

        nbits = max(1, int(np.ceil(np.log2(k_ref.shape[1]))))
        return lax.fori_loop(11 - nbits, 11, cut_step, jnp.zeros((1, t), jnp.int32))

    take_all = lambda: jnp.full((1, t), k_ref.shape[1], jnp.int32)
    cut = lax.cond(jnp.max(n_ties - need) > 0.0, tie_cut, take_all)

    def bias_chunk(c, _):
        kc = key_ref[c]
        keep = visible(c) & ((kc > thr) | ((kc == thr) & (kpos + c * t <= cut)))
        bias_ref[c] = jnp.where(keep, 0.0, NEG)
        return 0

    lax.fori_loop(0, nc, bias_chunk, 0)

    q = q_ref[0]
    qh = []
    for hd in range(DSA_HEADS):
        g, e = divmod(hd, 2)
        grp = q[:, LANES * g:LANES * (g + 1)]
        qh.append(jnp.where((lane >= 64 * e) & (lane < 64 * (e + 1)), grp, jnp.zeros_like(grp)))

    def scores(c):
        start = pl.multiple_of(c * t, t)
        ks = k_ref[0, pl.ds(start, t), :]
        sts = [_dot_nt(ks, qh[hd]) for hd in range(DSA_HEADS)]
        return tuple(st + bias_ref[c] for st in sts)

    init = tuple(_softmax_init(t, DSA_DIM) for _ in range(DSA_HEADS))
    stats = lax.fori_loop(
        0, nc, lambda c, st: _softmax_steps(scores(c), st, [vt_ref[0, c]] * DSA_HEADS), init)
    for g in range(DSA_HEADS // 2):
        o_ref[0, :, LANES * g:LANES * (g + 1)] = _pair_output(stats[2 * g:2 * g + 2])


def _dsa(q, k, vt, qi, ki, wit, n_sel):
    b, s, _ = q.shape
    t = ATT_T
    tile = lambda w: pl.BlockSpec((1, t, w), lambda bi, i: (bi, i, 0))
    seq = lambda w: pl.BlockSpec((1, s, w), lambda bi, i: (bi, 0, 0))
    return pl.pallas_call(
        functools.partial(_dsa_kernel, n_sel=n_sel),
        grid=(b, s // t),
        in_specs=[tile(256), seq(LANES), pl.BlockSpec((1, s // t, DSA_DIM, t), lambda bi, i: (bi, 0, 0, 0)),
                  tile(256), seq(LANES), pl.BlockSpec((1, 1, IDX_HEADS, t), lambda bi, i: (bi, i, 0, 0))],
        out_specs=tile(256),
        out_shape=jax.ShapeDtypeStruct((b, s, 256), BF16),
        scratch_shapes=[pltpu.VMEM((s // t, t, t), jnp.int32), pltpu.VMEM((s // t, t, t), F32),
                        pltpu.VMEM((s // t, t, t), BF16)],
        compiler_params=_params("parallel", "arbitrary"),
        name="dsa",
    )(q, k, vt, qi, ki, wit)


def _memkv_kernel(m_ref, g_ref, w_ref, k_ref, v_ref):
    hm = _rmsnorm(m_ref[0], g_ref[...]).astype(BF16)
    kv = _dot(hm, w_ref[...])
    half = kv.shape[1] // 2
    k_ref[0] = kv[:, :half].astype(BF16)
    v_ref[0] = kv[:, half:].astype(BF16)


def _memkv(mem, gain, wkv, layer):
    b, ml, d = mem.shape
    n = XA_HEADS * XA_DIM
    return pl.pallas_call(
        _memkv_kernel,
        grid=(b,),
        in_specs=[pl.BlockSpec((1, ml, d), lambda bi: (bi, 0, 0)),
                  _vec_spec(d, layer),
                  pl.BlockSpec((None, d, 2 * n), lambda bi: (layer, 0, 0))],
        out_specs=[pl.BlockSpec((1, ml, n), lambda bi: (bi, 0, 0))] * 2,
        out_shape=[jax.ShapeDtypeStruct((b, ml, n), BF16)] * 2,
        compiler_params=_params("parallel"),
        name="memkv",
    )(mem, gain, wkv)


def _mix_kernel(x_ref, a_ref, b_ref, c_ref, wout_ref, g_ref, wq_ref, kx_ref, vx_ref, wo_ref, o_ref):
    mixed = jnp.concatenate([a_ref[0], b_ref[0], c_ref[0]], axis=-1)
    x1 = x_ref[0] + _dot(mixed, wout_ref[...])
    hq = _rmsnorm(x1, g_ref[...]).astype(BF16)
    q = (_dot(hq, wq_ref[...]) * (XA_DIM ** -0.5 * LOG2E)).astype(BF16)
    outs = []
    for hd in range(XA_HEADS):
        sl = slice(XA_DIM * hd, XA_DIM * (hd + 1))
        s = _dot_nt(q[:, sl], kx_ref[0, :, sl])
        pe = jnp.exp2(s - jnp.max(s, axis=-1, keepdims=True))
        o = _dot(pe.astype(BF16), vx_ref[0, :, sl]) / jnp.sum(pe, axis=-1, keepdims=True)
        outs.append(o.astype(BF16))
    o_ref[0] = x1 + _dot(jnp.concatenate(outs, axis=-1), wo_ref[...])


def _mix_xattn(x, a, bm, c, wout, gain, wq, kx, vx, wo, layer):
    b, s, d = x.shape
    tm = MIX_TM
    n = XA_HEADS * XA_DIM
    ml = kx.shape[1]
    tile = lambda w: pl.BlockSpec((1, tm, w), lambda bi, j: (bi, j, 0))
    full = lambda r, cdim: pl.BlockSpec((None, r, cdim), lambda bi, j: (layer, 0, 0))
    mem = pl.BlockSpec((1, ml, n), lambda bi, j: (bi, 0, 0))
    return pl.pallas_call(
        _mix_kernel,
        grid=(b, s // tm),
        in_specs=[tile(d), tile(a.shape[2]), tile(bm.shape[2]), tile(c.shape[2]), full(d, d),
                  _vec_spec(d, layer), full(d, n), mem, mem, full(n, d)],
        out_specs=tile(d),
        out_shape=jax.ShapeDtypeStruct((b, s, d), F32),
        compiler_params=_params("parallel", "parallel"),
        name="mix_xattn",
    )(x, a, bm, c, wout, gain, wq, kx, vx, wo)


def _rope_tables(s, d):
    half = d // 2
    inv = ROPE_THETA ** (-jnp.arange(half, dtype=F32) / half)
    ang = jnp.arange(s).astype(F32)[:, None] * inv[None, :]
    cos, sin = jnp.cos(ang), jnp.sin(ang)
    reps = LANES // d
    return (jnp.tile(jnp.concatenate([cos, cos], -1), (1, reps)),
            jnp.tile(jnp.concatenate([-sin, sin], -1), (1, reps)))


def _layout_w_in(w_in):
    sizes = (384, 384, 384, FOX_HEADS, MLA_Q_RANK, MLA_KV_RANK, MLA_ROPE,
             DSA_HEADS * DSA_DIM, DSA_DIM, DSA_DIM, IDX_HEADS * IDX_DIM, IDX_DIM, IDX_HEADS)
    offs = np.concatenate([[0], np.cumsum(sizes)])
    fq, fk, fv, ff, mcq, mckv, mkr, dq, dk, dv, dqi, dki, dwi = (
        w_in[:, :, offs[n]:offs[n + 1]] for n in range(len(sizes)))
    zeros = lambda w: jnp.zeros(w_in.shape[:2] + (w,), w_in.dtype)
    pad = lambda a: jnp.concatenate([a, zeros(LANES - a.shape[2])], -1)
    out = jnp.concatenate([
        fq, fk, fv, pad(ff), mcq, mckv,
        jnp.concatenate([mkr, mkr, zeros(64)], -1),
        dq, jnp.concatenate([dk, dk], -1), jnp.concatenate([dv, dv], -1),
        dqi, jnp.concatenate([dki] * 4, -1), pad(dwi)], -1)
    assert out.shape[2] == N_PROJ
    return out.astype(BF16)


def _layout_w_uq(w_uq):
    hw = MLA_NOPE + MLA_ROPE
    cols = []
    for pr in range(MLA_HEADS // 2):
        h0, h1 = 2 * pr, 2 * pr + 1
        cols += [w_uq[:, :, hw * h0:hw * h0 + MLA_NOPE], w_uq[:, :, hw * h1:hw * h1 + MLA_NOPE],
                 w_uq[:, :, hw * h0 + MLA_NOPE:hw * (h0 + 1)], w_uq[:, :, hw * h1 + MLA_NOPE:hw * (h1 + 1)],
                 jnp.zeros(w_uq.shape[:2] + (64,), w_uq.dtype)]
    return jnp.concatenate(cols, -1).astype(BF16)


def _layout_w_ukv(w_ukv):
    hw = MLA_NOPE + MLA_V
    kn = [w_ukv[:, :, hw * hd:hw * hd + MLA_NOPE] for hd in range(MLA_HEADS)]
    vv = [w_ukv[:, :, hw * hd + MLA_NOPE:hw * (hd + 1)] for hd in range(MLA_HEADS)]
    return jnp.concatenate(kn + vv, -1).astype(BF16)


def kernel(x, mem, ffn1_norm, ffn1_wi, ffn1_wo, mix_norm, w_in, b_forget, mla_q_norm, mla_w_uq,
           mla_kv_norm, mla_w_ukv, w_out, xa_norm, mem_norm, xa_wq, xa_wkv, xa_wo,
           ffn2_norm, ffn2_wi, ffn2_wo, final_norm):
    b, s, d = x.shape
    depth = w_in.shape[0]
    assert s % ATT_T == 0 and s % MIX_TM == 0 and (b * s) % FFN_TM == 0 and s <= 2048
    n_sel = min(TOPK_MAX, s // 4)
    assert PROJ_TM == ATT_T and n_sel <= ATT_T

    win = _layout_w_in(w_in)
    wuq = _layout_w_uq(mla_w_uq)
    wukv = _layout_w_ukv(mla_w_ukv)
    rows = lambda v: v.reshape(-1, 1, v.shape[-1])
    bfp = rows(jnp.concatenate([b_forget, jnp.zeros((depth, LANES - FOX_HEADS), F32)], -1))
    ffn1_norm, mix_norm, mla_q_norm, mla_kv_norm, xa_norm, mem_norm, ffn2_norm, final_norm = map(
        rows, (ffn1_norm, mix_norm, mla_q_norm, mla_kv_norm, xa_norm, mem_norm, ffn2_norm, final_norm))
    tabs = _rope_tables(s, IDX_DIM) + _rope_tables(s, DSA_DIM)
    w1i, w1o = ffn1_wi.astype(BF16), ffn1_wo.astype(BF16)
    w2i, w2o = ffn2_wi.astype(BF16), ffn2_wo.astype(BF16)
    wout, wq, wkv, wo = (w.astype(BF16) for w in (w_out, xa_wq, xa_wkv, xa_wo))

    for layer in range(depth):
        x = _ffn(x.reshape(b * s, d), ffn1_norm, w1i, w1o, layer).reshape(b, s, d)
        (fq, fk, fv, cq, ck, mq, mk, mv, dq, dk, dv, qi, ki, wi) = _proj(
            x, mix_norm, win, bfp, mla_q_norm, wuq, mla_kv_norm, wukv, tabs, layer)
        a_out = _attn(fq, fk, fv, cq, ck, kw=LANES, name="fox_attn")
        b_out = _attn(mq, mk, mv, kw=2 * LANES, name="mla_attn")
        c_out = _dsa(dq, dk, dv, qi, ki, wi, n_sel)
        kx, vx = _memkv(mem, mem_norm, wkv, layer)
        x = _mix_xattn(x, a_out, b_out, c_out, wout, xa_norm, wq, kx, vx, wo, layer)
        last = layer == depth - 1
        x = _ffn(x.reshape(b * s, d), ffn2_norm, w2i, w2o, layer,
                 final_gain=final_norm if last else None).reshape(b, s, d)
    return x
```

```python
import functools

import jax
import jax.numpy as jnp
import numpy as np
from jax import lax
from jax.experimental import pallas as pl
from jax.experimental.pallas import tpu as pltpu

F32 = jnp.float32
BF16 = jnp.bfloat16

ROPE_THETA = 10000.0
NORM_EPS = 1e-6
FOX_HEADS, FOX_DIM = 6, 64
MLA_HEADS, MLA_NOPE, MLA_ROPE, MLA_V = 6, 64, 32, 64
MLA_Q_RANK, MLA_KV_RANK = 256, 128
DSA_HEADS, DSA_DIM = 4, 64
IDX_HEADS, IDX_DIM = 8, 32
TOPK_MAX = 256
XA_HEADS, XA_DIM = 4, 128

LANES = 128
VMEM_LIMIT = 56 * 1024 * 1024
NEG = -1e30
LOG2E = 1.4426950408889634
INT_MIN = -(2 ** 31)
DIGIT_BITS = (14, 14, 4)
DIGIT_BIAS = 0x80

O_FQ, O_FK, O_FV, O_FF = 0, 384, 768, 1152
O_MCQ, O_MCKV, O_MKR = 1280, 1536, 1664
O_DQ, O_DK, O_DV, O_DQI, O_DKI, O_DWI = 1792, 2048, 2176, 2304, 2560, 2688
N_PROJ = 2816

FFN_TM, FFN_TF = 1024, 256
PROJ_TM = 256
ATT_T = 256
STRIP = 64
MIX_TM = 512


def _dot(a, b):
    return jnp.dot(a, b, preferred_element_type=F32)


def _dot_nt(a, b):
    return lax.dot_general(a, b, (((1,), (1,)), ((), ())), preferred_element_type=F32)


def _rmsnorm(x, g):
    return x * lax.rsqrt(jnp.mean(x * x, axis=-1, keepdims=True) + NORM_EPS) * g


def _vec_spec(width, layer):
    return pl.BlockSpec((None, 1, width), lambda *_: (layer, 0, 0))


def _params(*sem):
    return pltpu.CompilerParams(dimension_semantics=sem, vmem_limit_bytes=VMEM_LIMIT)


def _ffn_kernel(x_ref, g_ref, wi_ref, wo_ref, fg_ref, o_ref, h_ref, act_ref, *, final_norm):
    f = wo_ref.shape[0]
    h_ref[...] = _rmsnorm(x_ref[...], g_ref[...]).astype(BF16)
    for j in range(f // FFN_TF):
        lo, hi = j * FFN_TF, (j + 1) * FFN_TF
        gate = _dot(h_ref[...], wi_ref[:, lo:hi])
        up = _dot(h_ref[...], wi_ref[:, f + lo:f + hi])
        act_ref[:, lo:hi] = (gate * jax.nn.sigmoid(gate) * up).astype(BF16)
    y = x_ref[...] + 0.5 * _dot(act_ref[...], wo_ref[...])
    if final_norm:
        y = _rmsnorm(y, fg_ref[...])
    o_ref[...] = y


def _ffn(x2d, gain, wi, wo, layer, final_gain=None):
    m, d = x2d.shape
    f = wo.shape[1]
    assert f % FFN_TF == 0
    fg = gain if final_gain is None else final_gain
    resident = lambda r, c: pl.BlockSpec((None, r, c), lambda i: (layer, 0, 0), pipeline_mode=pl.Buffered(1))
    return pl.pallas_call(
        functools.partial(_ffn_kernel, final_norm=final_gain is not None),
        grid=(m // FFN_TM,),
        in_specs=[
            pl.BlockSpec((FFN_TM, d), lambda i: (i, 0)),
            _vec_spec(d, layer),
            resident(d, 2 * f),
            resident(f, d),
            _vec_spec(d, 0 if final_gain is not None else layer),
        ],
        out_specs=pl.BlockSpec((FFN_TM, d), lambda i: (i, 0)),
        out_shape=jax.ShapeDtypeStruct((m, d), F32),
        scratch_shapes=[pltpu.VMEM((FFN_TM, d), BF16), pltpu.VMEM((FFN_TM, f), BF16)],
        compiler_params=_params("parallel"),
        name="ffn",
    )(x2d, gain, wi, wo, fg)


def _rope128(x, cos, sin, half):
    lane = lax.broadcasted_iota(jnp.int32, x.shape, 1)
    first = (lane % (2 * half)) < half
    partner = jnp.where(first, pltpu.roll(x, LANES - half, 1), pltpu.roll(x, half, 1))
    return x * cos + partner * sin


def _proj_kernel(x_ref, g_ref, win_ref, bf_ref, gq_ref, wuq_ref, gkv_ref, wukv_ref,
                 c32_ref, s32_ref, c64_ref, s64_ref,
                 fq_ref, fk_ref, fv_ref, cq_ref, ck_ref,
                 mq_ref, mk_ref, mv_ref,
                 dq_ref, dk_ref, dv_ref, qi_ref, ki_ref, wi_ref,
                 carry_ref):
    j = pl.program_id(1)
    tm = x_ref.shape[1]

    @pl.when(j == 0)
    def _():
        carry_ref[...] = jnp.zeros_like(carry_ref)

    h = _rmsnorm(x_ref[0], g_ref[...]).astype(BF16)
    p = _dot(h, win_ref[...])

    def grp(off, width=LANES):
        return p[:, off:off + width]

    c32, s32 = c32_ref[...], s32_ref[...]
    c64, s64 = c64_ref[...], s64_ref[...]

    fq_ref[0] = (grp(O_FQ, 384) * (FOX_DIM ** -0.5 * LOG2E)).astype(BF16)
    fk_ref[0] = grp(O_FK, 384).astype(BF16)
    fv_ref[0, 0] = grp(O_FV, 384).T.astype(BF16)

    z = grp(O_FF) + bf_ref[...]
    logf = jnp.minimum(z, 0.0) - jnp.log(1.0 + jnp.exp(-jnp.abs(z)))
    hi = logf.astype(BF16)
    r1 = logf - hi.astype(F32)
    mid = r1.astype(BF16)
    lo = (r1 - mid.astype(F32)).astype(BF16)
    row = lax.broadcasted_iota(jnp.int32, (tm, tm), 0)
    col = lax.broadcasted_iota(jnp.int32, (tm, tm), 1)
    tri = jnp.where(row >= col, 1.0, 0.0).astype(BF16)
    cum = (_dot(tri, hi) + _dot(tri, mid)) + _dot(tri, lo) + carry_ref[0:1, :]
    carry_ref[0:1, :] = cum[tm - 1:tm, :]
    cum2 = cum * LOG2E
    cq_ref[0] = cum2
    ck_ref[0, 0] = cum2.T[0:8, :]

    qn = _rmsnorm(grp(O_MCQ, MLA_Q_RANK), gq_ref[...]).astype(BF16)
    qm = _dot(qn, wuq_ref[...]) * ((MLA_NOPE + MLA_ROPE) ** -0.5 * LOG2E)
    kvn = _rmsnorm(grp(O_MCKV, MLA_KV_RANK), gkv_ref[...]).astype(BF16)
    kvm = _dot(kvn, wukv_ref[...])
    krt = _rope128(grp(O_MKR), c32, s32, MLA_ROPE // 2).astype(BF16)
    for pr in range(MLA_HEADS // 2):
        mq_ref[0, :, 256 * pr:256 * pr + 128] = qm[:, 256 * pr:256 * pr + 128].astype(BF16)
        mq_ref[0, :, 256 * pr + 128:256 * pr + 256] = _rope128(
            qm[:, 256 * pr + 128:256 * pr + 256], c32, s32, MLA_ROPE // 2).astype(BF16)
        mk_ref[0, :, 256 * pr:256 * pr + 128] = kvm[:, 128 * pr:128 * pr + 128].astype(BF16)
        mk_ref[0, :, 256 * pr + 128:256 * pr + 256] = krt
    mv_ref[0, 0] = kvm[:, 384:768].T.astype(BF16)

    for gi in range(2):
        dq_ref[0, :, 128 * gi:128 * gi + 128] = (
            _rope128(grp(O_DQ + 128 * gi), c64, s64, DSA_DIM // 2) * (DSA_DIM ** -0.5 * LOG2E)).astype(BF16)
        qi_ref[0, :, 128 * gi:128 * gi + 128] = (
            _rope128(grp(O_DQI + 128 * gi), c32, s32, IDX_DIM // 2) * (IDX_DIM ** -0.5)).astype(BF16)
    dk_ref[0] = _rope128(grp(O_DK), c64, s64, DSA_DIM // 2).astype(BF16)
    dv_ref[0, 0] = grp(O_DV).T[0:DSA_DIM, :].astype(BF16)
    ki_ref[0] = _rope128(grp(O_DKI), c32, s32, IDX_DIM // 2).astype(BF16)
    wi_ref[0, 0] = (grp(O_DWI) * (IDX_HEADS ** -0.5)).T[0:IDX_HEADS, :]


def _proj(x, gain, win, bfp, gq, wuq, gkv, wukv, tabs, layer):
    b, s, d = x.shape
    tm = PROJ_TM
    nt = s // tm
    tok = lambda w, dt: jax.ShapeDtypeStruct((b, s, w), dt)
    tspec = lambda w: pl.BlockSpec((1, tm, w), lambda bi, j: (bi, j, 0))
    full = lambda shape: pl.BlockSpec((None,) + shape, lambda bi, j: (layer,) + (0,) * len(shape))
    vec = lambda w: _vec_spec(w, layer)
    tab = pl.BlockSpec((tm, LANES), lambda bi, j: (j, 0))
    tr = lambda r, dt: jax.ShapeDtypeStruct((b, nt, r, tm), dt)
    trspec = lambda r: pl.BlockSpec((1, 1, r, tm), lambda bi, j: (bi, j, 0, 0))
    out_shape = [tok(384, BF16), tok(384, BF16), tr(384, BF16), tok(LANES, F32), tr(8, F32),
                 tok(768, BF16), tok(768, BF16), tr(384, BF16),
                 tok(256, BF16), tok(LANES, BF16), tr(DSA_DIM, BF16),
                 tok(256, BF16), tok(LANES, BF16), tr(IDX_HEADS, F32)]
    out_specs = [tspec(384), tspec(384), trspec(384), tspec(LANES), trspec(8),
                 tspec(768), tspec(768), trspec(384),
                 tspec(256), tspec(LANES), trspec(DSA_DIM),
                 tspec(256), tspec(LANES), trspec(IDX_HEADS)]
    return pl.pallas_call(
        _proj_kernel,
        grid=(b, nt),
        in_specs=[tspec(d), vec(d), full((d, N_PROJ)), vec(LANES),
                  vec(MLA_Q_RANK), full((MLA_Q_RANK, 768)), vec(MLA_KV_RANK), full((MLA_KV_RANK, 768)),
                  tab, tab, tab, tab],
        out_specs=out_specs,
        out_shape=out_shape,
        scratch_shapes=[pltpu.VMEM((8, LANES), F32)],
        compiler_params=_params("arbitrary", "arbitrary"),
        name="proj",
    )(x, gain, win, bfp, gq, wuq, gkv, wukv, *tabs)


def _score_pass(st_ref, slot, k_chunks, qh, finish):
    sts = [_dot_nt(kc, q) for kc, q in zip(k_chunks, qh)]
    cmax = []
    for hd, st in enumerate(sts):
        mx = None
        for r in range(0, st.shape[0], STRIP):
            s = finish(hd, r, st[r:r + STRIP])
            st_ref[slot, hd, r:r + STRIP, :] = s
            pm = jnp.max(s, axis=0, keepdims=True)
            mx = pm if mx is None else jnp.maximum(mx, pm)
        cmax.append(mx)
    return tuple(cmax)


def _softmax_pass(st_ref, pt_ref, slot, cmax, stats, vts):
    pre = []
    for hd, (cm, (m, l, _)) in enumerate(zip(cmax, stats)):
        m_new = jnp.maximum(m, cm)
        alpha = jnp.exp2(m - m_new)
        ls = None
        for r in range(0, st_ref.shape[2], STRIP):
            p = jnp.exp2(st_ref[slot, hd, r:r + STRIP, :] - m_new)
            ps = jnp.sum(p, axis=0, keepdims=True)
            ls = ps if ls is None else ls + ps
            pt_ref[hd, r:r + STRIP, :] = p.astype(BF16)
        pre.append((m_new, alpha, alpha * l + ls))
    pvs = [_dot(vt, pt_ref[hd]) for hd, vt in enumerate(vts)]
    return tuple((m_new, l, alpha * acc + pv)
                 for (m_new, alpha, l), (_, _, acc), pv in zip(pre, stats, pvs))


def _pipelined_step(st_ref, pt_ref, k_chunks, qh, finish, cmax, stats, vts):
    pre, nxt = [], []
    for hd, (kc, q) in enumerate(zip(k_chunks, qh)):
        st_new = _dot_nt(kc, q)
        m, l, _ = stats[hd]
        m_new = jnp.maximum(m, cmax[hd])
        alpha = jnp.exp2(m - m_new)
        ls = None
        for r in range(0, st_new.shape[0], STRIP):
            p = jnp.exp2(st_ref[0, hd, r:r + STRIP, :] - m_new)
            ps = jnp.sum(p, axis=0, keepdims=True)
            ls = ps if ls is None else ls + ps
            pt_ref[hd, r:r + STRIP, :] = p.astype(BF16)
        pre.append((m_new, alpha, alpha * l + ls))
        mx = None
        for r in range(0, st_new.shape[0], STRIP):
            s = finish(hd, r, st_new[r:r + STRIP])
            st_ref[0, hd, r:r + STRIP, :] = s
            pm = jnp.max(s, axis=0, keepdims=True)
            mx = pm if mx is None else jnp.maximum(mx, pm)
        nxt.append(mx)
    pvs = [_dot(vt, pt_ref[hd]) for hd, vt in enumerate(vts)]
    return tuple(nxt), tuple((m_new, l, alpha * acc + pv)
                             for (m_new, alpha, l), (_, _, acc), pv in zip(pre, stats, pvs))


def _softmax_init(t, dv):
    return (jnp.full((1, t), NEG, F32), jnp.zeros((1, t), F32), jnp.zeros((dv, t), F32))


def _pair_output(carry):
    ot = jnp.concatenate([carry[0][2] / carry[0][1], carry[1][2] / carry[1][1]], axis=0)
    return ot.T.astype(BF16)


def _attn_kernel(*refs, n_pairs, kw, has_bias):
    if has_bias:
        q_ref, k_ref, vt_ref, ccol_ref, crow_ref, o_ref, st_ref, pt_ref = refs
    else:
        q_ref, k_ref, vt_ref, o_ref, st_ref, pt_ref = refs
    i = pl.program_id(1)
    t = q_ref.shape[1]
    lane = lax.broadcasted_iota(jnp.int32, (t, kw), 1)
    kpos = lax.broadcasted_iota(jnp.int32, (STRIP, t), 0)
    qpos = lax.broadcasted_iota(jnp.int32, (STRIP, t), 1)

    qh = []
    for pr in range(n_pairs):
        qs = q_ref[0, :, kw * pr:kw * (pr + 1)]
        for e in range(2):
            sel = (lane >= 64 * e) & (lane < 64 * e + 64)
            if kw == 256:
                sel = sel | ((lane >= 128 + 32 * e) & (lane < 160 + 32 * e))
            qh.append(jnp.where(sel, qs, jnp.zeros_like(qs)))

    heads = range(2 * n_pairs)

    def chunk(c, diag):
        start = pl.multiple_of(c * t, t)

        def finish(hd, r, s):
            if has_bias:
                s = s + (crow_ref[0, 0, hd:hd + 1, :] - ccol_ref[0, pl.ds(start + r, STRIP), hd:hd + 1])
            if diag:
                s = jnp.where(kpos + r <= qpos, s, NEG)
            return s

        return [k_ref[0, pl.ds(start, t), kw * (hd // 2):kw * (hd // 2 + 1)] for hd in heads], finish

    def values(c):
        return [vt_ref[0, c, 64 * hd:64 * (hd + 1), :] for hd in heads]

    def step(j, carry):
        cmax, stats = carry
        kcs, finish = chunk(j, False)
        return _pipelined_step(st_ref, pt_ref, kcs, qh, finish, cmax, stats,
                               values(jnp.where(j == 0, i, j - 1)))

    kcs, finish = chunk(i, True)
    init = (_score_pass(st_ref, 0, kcs, qh, finish), tuple(_softmax_init(t, 64) for _ in heads))
    cmax, stats = lax.fori_loop(0, i, step, init)
    stats = _softmax_pass(st_ref, pt_ref, 0, cmax, stats, values(jnp.maximum(i - 1, 0)))
    for pr in range(n_pairs):
        o_ref[0, :, LANES * pr:LANES * (pr + 1)] = _pair_output(stats[2 * pr:2 * pr + 2])


def _attn(q, k, vt, ccol=None, crow=None, *, kw, name):
    b, s, qw = q.shape
    n_pairs = qw // kw
    t = ATT_T
    has_bias = ccol is not None
    in_specs = [pl.BlockSpec((1, t, qw), lambda bi, i: (bi, i, 0)),
                pl.BlockSpec((1, s, qw), lambda bi, i: (bi, 0, 0)),
                pl.BlockSpec((1, s // t, n_pairs * LANES, t), lambda bi, i: (bi, 0, 0, 0))]
    args = [q, k, vt]
    if has_bias:
        in_specs += [pl.BlockSpec((1, s, LANES), lambda bi, i: (bi, 0, 0)),
                     pl.BlockSpec((1, 1, 8, t), lambda bi, i: (bi, i, 0, 0))]
        args += [ccol, crow]
    return pl.pallas_call(
        functools.partial(_attn_kernel, n_pairs=n_pairs, kw=kw, has_bias=has_bias),
        grid=(b, s // t),
        in_specs=in_specs,
        out_specs=pl.BlockSpec((1, t, n_pairs * LANES), lambda bi, i: (bi, i, 0)),
        out_shape=jax.ShapeDtypeStruct((b, s, n_pairs * LANES), BF16),
        scratch_shapes=[pltpu.VMEM((1, 2 * n_pairs, t, t), F32), pltpu.VMEM((2 * n_pairs, t, t), BF16)],
        compiler_params=_params("parallel", "arbitrary"),
        name=name,
    )(*args)


def _sortable(x):
    bits = lax.bitcast_convert_type(x + 0.0, jnp.int32)
    return jnp.where(bits < 0, bits ^ 0x7FFFFFFF, bits)


def _dsa_kernel(q_ref, k_ref, vt_ref, qi_ref, ki_ref, wit_ref, o_ref, key_ref, bias_ref, dig_ref, st_ref, pt_ref, *, n_sel):
    i = pl.program_id(1)
    t = q_ref.shape[1]
    nc = i + 1
    lane = lax.broadcasted_iota(jnp.int32, (t, LANES), 1)
    kpos = lax.broadcasted_iota(jnp.int32, (t, t), 0)
    qpos = lax.broadcasted_iota(jnp.int32, (t, t), 1)

    def visible(c):
        return (kpos - qpos) <= (i - c) * t

    qi = qi_ref[0]
    qih, wih = [], []
    for hd in range(IDX_HEADS):
        g, e = divmod(hd, LANES // IDX_DIM)
        grp = qi[:, LANES * g:LANES * (g + 1)]
        qih.append(jnp.where((lane >= IDX_DIM * e) & (lane < IDX_DIM * (e + 1)), grp, jnp.zeros_like(grp)))
        wih.append(wit_ref[0, 0, hd:hd + 1, :])

    def score_chunk(c, _):
        start = pl.multiple_of(c * t, t)
        kic = ki_ref[0, pl.ds(start, t), :]
        isc = jnp.zeros((t, t), F32)
        for hd in range(IDX_HEADS):
            isc = isc + jnp.maximum(_dot_nt(kic, qih[hd]), 0.0) * wih[hd]
        key_ref[c] = jnp.where(visible(c), _sortable(isc), INT_MIN)
        return 0

    lax.fori_loop(0, nc, score_chunk, 0)

    def as_pattern(d):
        return lax.bitcast_convert_type(lax.shift_left(d, 16), F32).astype(BF16)

    def build_digits(shift, nbits, prefix_shift, prefix):
        def body(c, _):
            kc = key_ref[c]
            u = kc ^ INT_MIN
            member = kc != INT_MIN
            if prefix is not None:
                member = member & (lax.shift_right_logical(u, prefix_shift) == prefix)
            d = lax.shift_right_logical(u, shift) & ((1 << nbits) - 1)
            dig_ref[c] = as_pattern(jnp.where(member, d + DIGIT_BIAS, 0))
            return 0
        lax.fori_loop(0, nc, body, 0)

    one, zero = jnp.ones((), BF16), jnp.zeros((), BF16)

    def count_digits(pred):
        def body(c, accs):
            w = jnp.where(pred(dig_ref[c]), one, zero)
            accs = list(accs)
            for r in range(t // 16):
                accs[r % 2] = accs[r % 2] + w[16 * r:16 * (r + 1)]
            return tuple(accs)
        a0, a1 = lax.fori_loop(0, nc, body, (jnp.zeros((16, t), BF16),) * 2)
        return jnp.sum(a0.astype(F32) + a1.astype(F32), axis=0, keepdims=True)

    def search_digit(nbits, n_above):
        def step(b, dg):
            cand = dg | lax.shift_left(jnp.int32(1), nbits - 1 - b)
            cpat = as_pattern(cand + DIGIT_BIAS)
            cnt = n_above + count_digits(lambda x: x >= cpat)
            return jnp.where(cnt >= n_sel, cand, dg)
        dg = lax.fori_loop(0, nbits, step, jnp.zeros((1, t), jnp.int32))
        dpat = as_pattern(dg + DIGIT_BIAS)
        return dg, dpat, n_above + count_digits(lambda x: x > dpat)

    prefix, n_gt, low = None, jnp.zeros((1, t), F32), 32
    for nbits in DIGIT_BITS:
        build_digits(low - nbits, nbits, low, prefix)
        low -= nbits
        dg, dpat, n_gt = search_digit(nbits, n_gt)
        prefix = dg if prefix is None else lax.shift_left(prefix, nbits) | dg
    thr = prefix ^ INT_MIN

    need = n_sel - n_gt
    n_ties = count_digits(lambda x: x == dpat)

    def tie_cut():
        def count(pred):
            def body(c, cnt):
                hit = pred(key_ref[c], kpos + c * t)
                return cnt + jnp.sum(jnp.where(hit, 1.0, 0.0), axis=0, keepdims=True)
            return lax.fori_loop(0, nc, body, jnp.zeros((1, t), F32))

        def cut_step(b, cut):
            cand = cut | lax.shift_left(jnp.int32(1), 10 - b)
            cnt = count(lambda kc, idx: (kc == thr) & (idx < cand))
            return jnp.where(cnt < need, cand, cut)

        nbits = max(1, int(np.ceil(np.log2(k_ref.shape[1]))))
        return lax.fori_loop(11 - nbits, 11, cut_step, jnp.zeros((1, t), jnp.int32))

    take_all = lambda: jnp.full((1, t), k_ref.shape[1], jnp.int32)
    cut = lax.cond(jnp.max(n_ties - need) > 0.0, tie_cut, take_all)

    def bias_chunk(c, _):
        kc = key_ref[c]
        keep = visible(c) & ((kc > thr) | ((kc == thr) & (kpos + c * t <= cut)))
        bias_ref[c] = jnp.where(keep, 0.0, NEG)
        return 0

    lax.fori_loop(0, nc, bias_chunk, 0)

    q = q_ref[0]
    qh = []
    for hd in range(DSA_HEADS):
        g, e = divmod(hd, 2)
        grp = q[:, LANES * g:LANES * (g + 1)]
        qh.append(jnp.where((lane >= 64 * e) & (lane < 64 * (e + 1)), grp, jnp.zeros_like(grp)))

    def scores(c):
        start = pl.multiple_of(c * t, t)
        ks = k_ref[0, pl.ds(start, t), :]
        return [ks] * DSA_HEADS, lambda hd, r, s: s + bias_ref[c, r:r + STRIP, :]

    def step(j, carry):
        cmax, stats = carry
        kcs, finish = scores(j)
        return _pipelined_step(st_ref, pt_ref, kcs, qh, finish, cmax, stats, [vt_ref[0, j - 1]] * DSA_HEADS)

    kcs, finish = scores(0)
    init = (_score_pass(st_ref, 0, kcs, qh, finish), tuple(_softmax_init(t, DSA_DIM) for _ in range(DSA_HEADS)))
    cmax, stats = lax.fori_loop(1, nc, step, init)
    stats = _softmax_pass(st_ref, pt_ref, 0, cmax, stats, [vt_ref[0, nc - 1]] * DSA_HEADS)
    for g in range(DSA_HEADS // 2):
        o_ref[0, :, LANES * g:LANES * (g + 1)] = _pair_output(stats[2 * g:2 * g + 2])


def _dsa(q, k, vt, qi, ki, wit, n_sel):
    b, s, _ = q.shape
    t = ATT_T
    tile = lambda w: pl.BlockSpec((1, t, w), lambda bi, i: (bi, i, 0))
    seq = lambda w: pl.BlockSpec((1, s, w), lambda bi, i: (bi, 0, 0))
    return pl.pallas_call(
        functools.partial(_dsa_kernel, n_sel=n_sel),
        grid=(b, s // t),
        in_specs=[tile(256), seq(LANES), pl.BlockSpec((1, s // t, DSA_DIM, t), lambda bi, i: (bi, 0, 0, 0)),
                  tile(256), seq(LANES), pl.BlockSpec((1, 1, IDX_HEADS, t), lambda bi, i: (bi, i, 0, 0))],
        out_specs=tile(256),
        out_shape=jax.ShapeDtypeStruct((b, s, 256), BF16),
        scratch_shapes=[pltpu.VMEM((s // t, t, t), jnp.int32), pltpu.VMEM((s // t, t, t), F32),
                        pltpu.VMEM((s // t, t, t), BF16),
                        pltpu.VMEM((1, DSA_HEADS, t, t), F32), pltpu.VMEM((DSA_HEADS, t, t), BF16)],
        compiler_params=_params("parallel", "arbitrary"),
        name="dsa",
    )(q, k, vt, qi, ki, wit)


def _memkv_kernel(m_ref, g_ref, w_ref, k_ref, v_ref):
    hm = _rmsnorm(m_ref[0], g_ref[...]).astype(BF16)
    kv = _dot(hm, w_ref[...])
    half = kv.shape[1] // 2
    k_ref[0] = kv[:, :half].astype(BF16)
    v_ref[0] = kv[:, half:].astype(BF16)


def _memkv(mem, gain, wkv, layer):
    b, ml, d = mem.shape
    n = XA_HEADS * XA_DIM
    return pl.pallas_call(
        _memkv_kernel,
        grid=(b,),
        in_specs=[pl.BlockSpec((1, ml, d), lambda bi: (bi, 0, 0)),
                  _vec_spec(d, layer),
                  pl.BlockSpec((None, d, 2 * n), lambda bi: (layer, 0, 0))],
        out_specs=[pl.BlockSpec((1, ml, n), lambda bi: (bi, 0, 0))] * 2,
        out_shape=[jax.ShapeDtypeStruct((b, ml, n), BF16)] * 2,
        compiler_params=_params("parallel"),
        name="memkv",
    )(mem, gain, wkv)


def _mix_kernel(x_ref, a_ref, b_ref, c_ref, wout_ref, g_ref, wq_ref, kx_ref, vx_ref, wo_ref, o_ref):
    mixed = jnp.concatenate([a_ref[0], b_ref[0], c_ref[0]], axis=-1)
    x1 = x_ref[0] + _dot(mixed, wout_ref[...])
    hq = _rmsnorm(x1, g_ref[...]).astype(BF16)
    q = (_dot(hq, wq_ref[...]) * (XA_DIM ** -0.5 * LOG2E)).astype(BF16)
    outs = []
    for hd in range(XA_HEADS):
        sl = slice(XA_DIM * hd, XA_DIM * (hd + 1))
        s = _dot_nt(q[:, sl], kx_ref[0, :, sl])
        pe = jnp.exp2(s - jnp.max(s, axis=-1, keepdims=True))
        o = _dot(pe.astype(BF16), vx_ref[0, :, sl]) / jnp.sum(pe, axis=-1, keepdims=True)
        outs.append(o.astype(BF16))
    o_ref[0] = x1 + _dot(jnp.concatenate(outs, axis=-1), wo_ref[...])


def _mix_xattn(x, a, bm, c, wout, gain, wq, kx, vx, wo, layer):
    b, s, d = x.shape
    tm = MIX_TM
    n = XA_HEADS * XA_DIM
    ml = kx.shape[1]
    tile = lambda w: pl.BlockSpec((1, tm, w), lambda bi, j: (bi, j, 0))
    full = lambda r, cdim: pl.BlockSpec((None, r, cdim), lambda bi, j: (layer, 0, 0))
    mem = pl.BlockSpec((1, ml, n), lambda bi, j: (bi, 0, 0))
    return pl.pallas_call(
        _mix_kernel,
        grid=(b, s // tm),
        in_specs=[tile(d), tile(a.shape[2]), tile(bm.shape[2]), tile(c.shape[2]), full(d, d),
                  _vec_spec(d, layer), full(d, n), mem, mem, full(n, d)],
        out_specs=tile(d),
        out_shape=jax.ShapeDtypeStruct((b, s, d), F32),
        compiler_params=_params("parallel", "parallel"),
        name="mix_xattn",
    )(x, a, bm, c, wout, gain, wq, kx, vx, wo)


def _rope_tables(s, d):
    half = d // 2
    inv = ROPE_THETA ** (-jnp.arange(half, dtype=F32) / half)
    ang = jnp.arange(s).astype(F32)[:, None] * inv[None, :]
    cos, sin = jnp.cos(ang), jnp.sin(ang)
    reps = LANES // d
    return (jnp.tile(jnp.concatenate([cos, cos], -1), (1, reps)),
            jnp.tile(jnp.concatenate([-sin, sin], -1), (1, reps)))


def _layout_w_in(w_in):
    sizes = (384, 384, 384, FOX_HEADS, MLA_Q_RANK, MLA_KV_RANK, MLA_ROPE,
             DSA_HEADS * DSA_DIM, DSA_DIM, DSA_DIM, IDX_HEADS * IDX_DIM, IDX_DIM, IDX_HEADS)
    offs = np.concatenate([[0], np.cumsum(sizes)])
    fq, fk, fv, ff, mcq, mckv, mkr, dq, dk, dv, dqi, dki, dwi = (
        w_in[:, :, offs[n]:offs[n + 1]] for n in range(len(sizes)))
    zeros = lambda w: jnp.zeros(w_in.shape[:2] + (w,), w_in.dtype)
    pad = lambda a: jnp.concatenate([a, zeros(LANES - a.shape[2])], -1)
    out = jnp.concatenate([
        fq, fk, fv, pad(ff), mcq, mckv,
        jnp.concatenate([mkr, mkr, zeros(64)], -1),
        dq, jnp.concatenate([dk, dk], -1), jnp.concatenate([dv, dv], -1),
        dqi, jnp.concatenate([dki] * 4, -1), pad(dwi)], -1)
    assert out.shape[2] == N_PROJ
    return out.astype(BF16)


def _layout_w_uq(w_uq):
    hw = MLA_NOPE + MLA_ROPE
    cols = []
    for pr in range(MLA_HEADS // 2):
        h0, h1 = 2 * pr, 2 * pr + 1
        cols += [w_uq[:, :, hw * h0:hw * h0 + MLA_NOPE], w_uq[:, :, hw * h1:hw * h1 + MLA_NOPE],
                 w_uq[:, :, hw * h0 + MLA_NOPE:hw * (h0 + 1)], w_uq[:, :, hw * h1 + MLA_NOPE:hw * (h1 + 1)],
                 jnp.zeros(w_uq.shape[:2] + (64,), w_uq.dtype)]
    return jnp.concatenate(cols, -1).astype(BF16)


def _layout_w_ukv(w_ukv):
    hw = MLA_NOPE + MLA_V
    kn = [w_ukv[:, :, hw * hd:hw * hd + MLA_NOPE] for hd in range(MLA_HEADS)]
    vv = [w_ukv[:, :, hw * hd + MLA_NOPE:hw * (hd + 1)] for hd in range(MLA_HEADS)]
    return jnp.concatenate(kn + vv, -1).astype(BF16)


def kernel(x, mem, ffn1_norm, ffn1_wi, ffn1_wo, mix_norm, w_in, b_forget, mla_q_norm, mla_w_uq,
           mla_kv_norm, mla_w_ukv, w_out, xa_norm, mem_norm, xa_wq, xa_wkv, xa_wo,
           ffn2_norm, ffn2_wi, ffn2_wo, final_norm):
    b, s, d = x.shape
    depth = w_in.shape[0]
    assert s % ATT_T == 0 and s % MIX_TM == 0 and (b * s) % FFN_TM == 0 and s <= 2048
    n_sel = min(TOPK_MAX, s // 4)
    assert PROJ_TM == ATT_T and n_sel <= ATT_T

    win = _layout_w_in(w_in)
    wuq = _layout_w_uq(mla_w_uq)
    wukv = _layout_w_ukv(mla_w_ukv)
    rows = lambda v: v.reshape(-1, 1, v.shape[-1])
    bfp = rows(jnp.concatenate([b_forget, jnp.zeros((depth, LANES - FOX_HEADS), F32)], -1))
    ffn1_norm, mix_norm, mla_q_norm, mla_kv_norm, xa_norm, mem_norm, ffn2_norm, final_norm = map(
        rows, (ffn1_norm, mix_norm, mla_q_norm, mla_kv_norm, xa_norm, mem_norm, ffn2_norm, final_norm))
    tabs = _rope_tables(s, IDX_DIM) + _rope_tables(s, DSA_DIM)
    w1i, w1o = ffn1_wi.astype(BF16), ffn1_wo.astype(BF16)
    w2i, w2o = ffn2_wi.astype(BF16), ffn2_wo.astype(BF16)
    wout, wq, wkv, wo = (w.astype(BF16) for w in (w_out, xa_wq, xa_wkv, xa_wo))

    for layer in range(depth):
        x = _ffn(x.reshape(b * s, d), ffn1_norm, w1i, w1o, layer).reshape(b, s, d)
        (fq, fk, fv, cq, ck, mq, mk, mv, dq, dk, dv, qi, ki, wi) = _proj(
            x, mix_norm, win, bfp, mla_q_norm, wuq, mla_kv_norm, wukv, tabs, layer)
        a_out = _attn(fq, fk, fv, cq, ck, kw=LANES, name="fox_attn")
        b_out = _attn(mq, mk, mv, kw=2 * LANES, name="mla_attn")
        c_out = _dsa(dq, dk, dv, qi, ki, wi, n_sel)
        kx, vx = _memkv(mem, mem_norm, wkv, layer)
        x = _mix_xattn(x, a_out, b_out, c_out, wout, xa_norm, wq, kx, vx, wo, layer)
        last = layer == depth - 1
        x = _ffn(x.reshape(b * s, d), ffn2_norm, w2i, w2o, layer,
                 final_gain=final_norm if last else None).reshape(b, s, d)
    return x
```

```python
import functools

import jax
import jax.numpy as jnp
import numpy as np
from jax import lax
from jax.experimental import pallas as pl
from jax.experimental.pallas import tpu as pltpu

F32 = jnp.float32
BF16 = jnp.bfloat16

ROPE_THETA = 10000.0
NORM_EPS = 1e-6
FOX_HEADS, FOX_DIM = 6, 64
MLA_HEADS, MLA_NOPE, MLA_ROPE, MLA_V = 6, 64, 32, 64
MLA_Q_RANK, MLA_KV_RANK = 256, 128
DSA_HEADS, DSA_DIM = 4, 64
IDX_HEADS, IDX_DIM = 8, 32
TOPK_MAX = 256
XA_HEADS, XA_DIM = 4, 128

LANES = 128
VMEM_LIMIT = 56 * 1024 * 1024
NEG = -1e30
LOG2E = 1.4426950408889634
INT_MIN = -(2 ** 31)
DIGIT_BITS = (14, 14, 4)
DIGIT_BIAS = 0x80

O_FQ, O_FK, O_FV, O_FF = 0, 384, 768, 1152
O_MCQ, O_MCKV, O_MKR = 1280, 1536, 1664
O_DQ, O_DK, O_DV, O_DQI, O_DKI, O_DWI = 1792, 2048, 2176, 2304, 2560, 2688
N_PROJ = 2816

FFN_TM, FFN_TF = 1024, 256
PROJ_TM = 256
ATT_T = 256
STRIP = 64
MIX_TM = 512


def _dot(a, b):
    return jnp.dot(a, b, preferred_element_type=F32)


def _dot_nt(a, b):
    return lax.dot_general(a, b, (((1,), (1,)), ((), ())), preferred_element_type=F32)


def _rmsnorm(x, g):
    return x * lax.rsqrt(jnp.mean(x * x, axis=-1, keepdims=True) + NORM_EPS) * g


def _vec_spec(width, layer):
    return pl.BlockSpec((None, 1, width), lambda *_: (layer, 0, 0))


def _params(*sem):
    return pltpu.CompilerParams(dimension_semantics=sem, vmem_limit_bytes=VMEM_LIMIT)


def _ffn_kernel(x_ref, g_ref, wi_ref, wo_ref, fg_ref, o_ref, h_ref, act_ref, *, final_norm):
    f = wo_ref.shape[0]
    h_ref[...] = _rmsnorm(x_ref[...], g_ref[...]).astype(BF16)
    for j in range(f // FFN_TF):
        lo, hi = j * FFN_TF, (j + 1) * FFN_TF
        gate = _dot(h_ref[...], wi_ref[:, lo:hi])
        up = _dot(h_ref[...], wi_ref[:, f + lo:f + hi])
        act_ref[:, lo:hi] = (gate * jax.nn.sigmoid(gate) * up).astype(BF16)
    y = x_ref[...] + 0.5 * _dot(act_ref[...], wo_ref[...])
    if final_norm:
        y = _rmsnorm(y, fg_ref[...])
    o_ref[...] = y


def _ffn(x2d, gain, wi, wo, layer, final_gain=None):
    m, d = x2d.shape
    f = wo.shape[1]
    assert f % FFN_TF == 0
    fg = gain if final_gain is None else final_gain
    resident = lambda r, c: pl.BlockSpec((None, r, c), lambda i: (layer, 0, 0), pipeline_mode=pl.Buffered(1))
    return pl.pallas_call(
        functools.partial(_ffn_kernel, final_norm=final_gain is not None),
        grid=(m // FFN_TM,),
        in_specs=[
            pl.BlockSpec((FFN_TM, d), lambda i: (i, 0)),
            _vec_spec(d, layer),
            resident(d, 2 * f),
            resident(f, d),
            _vec_spec(d, 0 if final_gain is not None else layer),
        ],
        out_specs=pl.BlockSpec((FFN_TM, d), lambda i: (i, 0)),
        out_shape=jax.ShapeDtypeStruct((m, d), F32),
        scratch_shapes=[pltpu.VMEM((FFN_TM, d), BF16), pltpu.VMEM((FFN_TM, f), BF16)],
        compiler_params=_params("parallel"),
        name="ffn",
    )(x2d, gain, wi, wo, fg)


def _rope128(x, cos, sin, half):
    lane = lax.broadcasted_iota(jnp.int32, x.shape, 1)
    first = (lane % (2 * half)) < half
    partner = jnp.where(first, pltpu.roll(x, LANES - half, 1), pltpu.roll(x, half, 1))
    return x * cos + partner * sin


def _proj_kernel(x_ref, g_ref, win_ref, bf_ref, gq_ref, wuq_ref, gkv_ref, wukv_ref,
                 c32_ref, s32_ref, c64_ref, s64_ref,
                 fq_ref, fk_ref, fv_ref, cq_ref, ck_ref,
                 mq_ref, mk_ref, mv_ref,
                 dq_ref, dk_ref, dv_ref, qi_ref, ki_ref, wi_ref,
                 carry_ref):
    j = pl.program_id(1)
    tm = x_ref.shape[1]

    @pl.when(j == 0)
    def _():
        carry_ref[...] = jnp.zeros_like(carry_ref)

    h = _rmsnorm(x_ref[0], g_ref[...]).astype(BF16)
    p = _dot(h, win_ref[...])

    def grp(off, width=LANES):
        return p[:, off:off + width]

    c32, s32 = c32_ref[...], s32_ref[...]
    c64, s64 = c64_ref[...], s64_ref[...]

    fq_ref[0, 0] = (grp(O_FQ, 384) * (FOX_DIM ** -0.5 * LOG2E)).T.astype(BF16)
    fk_ref[0] = grp(O_FK, 384).astype(BF16)
    fv_ref[0, 0] = grp(O_FV, 384).T.astype(BF16)

    z = grp(O_FF) + bf_ref[...]
    logf = jnp.minimum(z, 0.0) - jnp.log(1.0 + jnp.exp(-jnp.abs(z)))
    hi = logf.astype(BF16)
    r1 = logf - hi.astype(F32)
    mid = r1.astype(BF16)
    lo = (r1 - mid.astype(F32)).astype(BF16)
    row = lax.broadcasted_iota(jnp.int32, (tm, tm), 0)
    col = lax.broadcasted_iota(jnp.int32, (tm, tm), 1)
    tri = jnp.where(row >= col, 1.0, 0.0).astype(BF16)
    cum = (_dot(tri, hi) + _dot(tri, mid)) + _dot(tri, lo) + carry_ref[0:1, :]
    carry_ref[0:1, :] = cum[tm - 1:tm, :]
    cum2 = cum * LOG2E
    cq_ref[0] = cum2
    ck_ref[0, 0] = cum2.T[0:8, :]

    qn = _rmsnorm(grp(O_MCQ, MLA_Q_RANK), gq_ref[...]).astype(BF16)
    qm = _dot(qn, wuq_ref[...]) * ((MLA_NOPE + MLA_ROPE) ** -0.5 * LOG2E)
    kvn = _rmsnorm(grp(O_MCKV, MLA_KV_RANK), gkv_ref[...]).astype(BF16)
    kvm = _dot(kvn, wukv_ref[...])
    krt = _rope128(grp(O_MKR), c32, s32, MLA_ROPE // 2).astype(BF16)
    for pr in range(MLA_HEADS // 2):
        mq_ref[0, 0, 256 * pr:256 * pr + 128, :] = qm[:, 256 * pr:256 * pr + 128].T.astype(BF16)
        mq_ref[0, 0, 256 * pr + 128:256 * pr + 256, :] = _rope128(
            qm[:, 256 * pr + 128:256 * pr + 256], c32, s32, MLA_ROPE // 2).T.astype(BF16)
        mk_ref[0, :, 256 * pr:256 * pr + 128] = kvm[:, 128 * pr:128 * pr + 128].astype(BF16)
        mk_ref[0, :, 256 * pr + 128:256 * pr + 256] = krt
    mv_ref[0, 0] = kvm[:, 384:768].T.astype(BF16)

    for gi in range(2):
        dq_ref[0, 0, 128 * gi:128 * gi + 128, :] = (
            _rope128(grp(O_DQ + 128 * gi), c64, s64, DSA_DIM // 2) * (DSA_DIM ** -0.5 * LOG2E)).T.astype(BF16)
        qi_ref[0, 0, 128 * gi:128 * gi + 128, :] = (
            _rope128(grp(O_DQI + 128 * gi), c32, s32, IDX_DIM // 2) * (IDX_DIM ** -0.5)).T.astype(BF16)
    dk_ref[0] = _rope128(grp(O_DK), c64, s64, DSA_DIM // 2).astype(BF16)
    dv_ref[0, 0] = grp(O_DV).T[0:DSA_DIM, :].astype(BF16)
    ki_ref[0] = _rope128(grp(O_DKI), c32, s32, IDX_DIM // 2).astype(BF16)
    wi_ref[0, 0] = (grp(O_DWI) * (IDX_HEADS ** -0.5)).T[0:IDX_HEADS, :]


def _proj(x, gain, win, bfp, gq, wuq, gkv, wukv, tabs, layer):
    b, s, d = x.shape
    tm = PROJ_TM
    nt = s // tm
    tok = lambda w, dt: jax.ShapeDtypeStruct((b, s, w), dt)
    tspec = lambda w: pl.BlockSpec((1, tm, w), lambda bi, j: (bi, j, 0))
    full = lambda shape: pl.BlockSpec((None,) + shape, lambda bi, j: (layer,) + (0,) * len(shape))
    vec = lambda w: _vec_spec(w, layer)
    tab = pl.BlockSpec((tm, LANES), lambda bi, j: (j, 0))
    tr = lambda r, dt: jax.ShapeDtypeStruct((b, nt, r, tm), dt)
    trspec = lambda r: pl.BlockSpec((1, 1, r, tm), lambda bi, j: (bi, j, 0, 0))
    out_shape = [tr(384, BF16), tok(384, BF16), tr(384, BF16), tok(LANES, F32), tr(8, F32),
                 tr(768, BF16), tok(768, BF16), tr(384, BF16),
                 tr(256, BF16), tok(LANES, BF16), tr(DSA_DIM, BF16),
                 tr(256, BF16), tok(LANES, BF16), tr(IDX_HEADS, F32)]
    out_specs = [trspec(384), tspec(384), trspec(384), tspec(LANES), trspec(8),
                 trspec(768), tspec(768), trspec(384),
                 trspec(256), tspec(LANES), trspec(DSA_DIM),
                 trspec(256), tspec(LANES), trspec(IDX_HEADS)]
    return pl.pallas_call(
        _proj_kernel,
        grid=(b, nt),
        in_specs=[tspec(d), vec(d), full((d, N_PROJ)), vec(LANES),
                  vec(MLA_Q_RANK), full((MLA_Q_RANK, 768)), vec(MLA_KV_RANK), full((MLA_KV_RANK, 768)),
                  tab, tab, tab, tab],
        out_specs=out_specs,
        out_shape=out_shape,
        scratch_shapes=[pltpu.VMEM((8, LANES), F32)],
        compiler_params=_params("arbitrary", "arbitrary"),
        name="proj",
    )(x, gain, win, bfp, gq, wuq, gkv, wukv, *tabs)


def _score_pass(st_ref, slot, k_chunks, qh, finish):
    sts = [_dot(kc, q) for kc, q in zip(k_chunks, qh)]
    cmax = []
    for hd, st in enumerate(sts):
        mx = None
        for r in range(0, st.shape[0], STRIP):
            s = finish(hd, r, st[r:r + STRIP])
            st_ref[slot, hd, r:r + STRIP, :] = s
            pm = jnp.max(s, axis=0, keepdims=True)
            mx = pm if mx is None else jnp.maximum(mx, pm)
        cmax.append(mx)
    return tuple(cmax)


def _softmax_pass(st_ref, pt_ref, slot, cmax, stats, vts):
    pre = []
    for hd, (cm, (m, l, _)) in enumerate(zip(cmax, stats)):
        m_new = jnp.maximum(m, cm)
        alpha = jnp.exp2(m - m_new)
        ls = None
        for r in range(0, st_ref.shape[2], STRIP):
            p = jnp.exp2(st_ref[slot, hd, r:r + STRIP, :] - m_new)
            ps = jnp.sum(p, axis=0, keepdims=True)
            ls = ps if ls is None else ls + ps
            pt_ref[hd, r:r + STRIP, :] = p.astype(BF16)
        pre.append((m_new, alpha, alpha * l + ls))
    pvs = [_dot(vt, pt_ref[hd]) for hd, vt in enumerate(vts)]
    return tuple((m_new, l, alpha * acc + pv)
                 for (m_new, alpha, l), (_, _, acc), pv in zip(pre, stats, pvs))


def _pipelined_step(st_ref, pt_ref, k_chunks, qh, finish, cmax, stats, vts):
    pre, nxt = [], []
    for hd, (kc, q) in enumerate(zip(k_chunks, qh)):
        st_new = _dot(kc, q)
        m, l, _ = stats[hd]
        m_new = jnp.maximum(m, cmax[hd])
        alpha = jnp.exp2(m - m_new)
        ls = None
        for r in range(0, st_new.shape[0], STRIP):
            p = jnp.exp2(st_ref[0, hd, r:r + STRIP, :] - m_new)
            ps = jnp.sum(p, axis=0, keepdims=True)
            ls = ps if ls is None else ls + ps
            pt_ref[hd, r:r + STRIP, :] = p.astype(BF16)
        pre.append((m_new, alpha, alpha * l + ls))
        mx = None
        for r in range(0, st_new.shape[0], STRIP):
            s = finish(hd, r, st_new[r:r + STRIP])
            st_ref[0, hd, r:r + STRIP, :] = s
            pm = jnp.max(s, axis=0, keepdims=True)
            mx = pm if mx is None else jnp.maximum(mx, pm)
        nxt.append(mx)
    pvs = [_dot(vt, pt_ref[hd]) for hd, vt in enumerate(vts)]
    return tuple(nxt), tuple((m_new, l, alpha * acc + pv)
                             for (m_new, alpha, l), (_, _, acc), pv in zip(pre, stats, pvs))


def _softmax_init(t, dv):
    return (jnp.full((1, t), NEG, F32), jnp.zeros((1, t), F32), jnp.zeros((dv, t), F32))


def _pair_output(carry):
    ot = jnp.concatenate([carry[0][2] / carry[0][1], carry[1][2] / carry[1][1]], axis=0)
    return ot.T.astype(BF16)


def _attn_kernel(*refs, n_pairs, kw, has_bias):
    if has_bias:
        q_ref, k_ref, vt_ref, ccol_ref, crow_ref, o_ref, st_ref, pt_ref = refs
    else:
        q_ref, k_ref, vt_ref, o_ref, st_ref, pt_ref = refs
    i = pl.program_id(1)
    t = q_ref.shape[3]
    feat = lax.broadcasted_iota(jnp.int32, (kw, t), 0)
    kpos = lax.broadcasted_iota(jnp.int32, (STRIP, t), 0)
    qpos = lax.broadcasted_iota(jnp.int32, (STRIP, t), 1)

    qh = []
    for pr in range(n_pairs):
        qs = q_ref[0, 0, kw * pr:kw * (pr + 1), :]
        for e in range(2):
            sel = (feat >= 64 * e) & (feat < 64 * e + 64)
            if kw == 256:
                sel = sel | ((feat >= 128 + 32 * e) & (feat < 160 + 32 * e))
            qh.append(jnp.where(sel, qs, jnp.zeros_like(qs)))

    heads = range(2 * n_pairs)

    def chunk(c, diag):
        start = pl.multiple_of(c * t, t)

        def finish(hd, r, s):
            if has_bias:
                s = s + (crow_ref[0, 0, hd:hd + 1, :] - ccol_ref[0, pl.ds(start + r, STRIP), hd:hd + 1])
            if diag:
                s = jnp.where(kpos + r <= qpos, s, NEG)
            return s

        return [k_ref[0, pl.ds(start, t), kw * (hd // 2):kw * (hd // 2 + 1)] for hd in heads], finish

    def values(c):
        return [vt_ref[0, c, 64 * hd:64 * (hd + 1), :] for hd in heads]

    def step(j, carry):
        cmax, stats = carry
        kcs, finish = chunk(j, False)
        return _pipelined_step(st_ref, pt_ref, kcs, qh, finish, cmax, stats,
                               values(jnp.where(j == 0, i, j - 1)))

    kcs, finish = chunk(i, True)
    init = (_score_pass(st_ref, 0, kcs, qh, finish), tuple(_softmax_init(t, 64) for _ in heads))
    cmax, stats = lax.fori_loop(0, i, step, init)
    stats = _softmax_pass(st_ref, pt_ref, 0, cmax, stats, values(jnp.maximum(i - 1, 0)))
    for pr in range(n_pairs):
        o_ref[0, :, LANES * pr:LANES * (pr + 1)] = _pair_output(stats[2 * pr:2 * pr + 2])


def _attn(qt, k, vt, ccol=None, crow=None, *, kw, name):
    b, s, qw = k.shape
    n_pairs = qw // kw
    t = ATT_T
    has_bias = ccol is not None
    in_specs = [pl.BlockSpec((1, 1, qw, t), lambda bi, i: (bi, i, 0, 0)),
                pl.BlockSpec((1, s, qw), lambda bi, i: (bi, 0, 0)),
                pl.BlockSpec((1, s // t, n_pairs * LANES, t), lambda bi, i: (bi, 0, 0, 0))]
    args = [qt, k, vt]
    if has_bias:
        in_specs += [pl.BlockSpec((1, s, LANES), lambda bi, i: (bi, 0, 0)),
                     pl.BlockSpec((1, 1, 8, t), lambda bi, i: (bi, i, 0, 0))]
        args += [ccol, crow]
    return pl.pallas_call(
        functools.partial(_attn_kernel, n_pairs=n_pairs, kw=kw, has_bias=has_bias),
        grid=(b, s // t),
        in_specs=in_specs,
        out_specs=pl.BlockSpec((1, t, n_pairs * LANES), lambda bi, i: (bi, i, 0)),
        out_shape=jax.ShapeDtypeStruct((b, s, n_pairs * LANES), BF16),
        scratch_shapes=[pltpu.VMEM((1, 2 * n_pairs, t, t), F32), pltpu.VMEM((2 * n_pairs, t, t), BF16)],
        compiler_params=_params("parallel", "arbitrary"),
        name=name,
    )(*args)


def _sortable(x):
    bits = lax.bitcast_convert_type(x + 0.0, jnp.int32)
    return jnp.where(bits < 0, bits ^ 0x7FFFFFFF, bits)


def _dsa_kernel(q_ref, k_ref, vt_ref, qi_ref, ki_ref, wit_ref, o_ref, key_ref, bias_ref, dig_ref, st_ref, pt_ref, *, n_sel):
    i = pl.program_id(1)
    t = q_ref.shape[3]
    nc = i + 1
    feat = lax.broadcasted_iota(jnp.int32, (LANES, t), 0)
    kpos = lax.broadcasted_iota(jnp.int32, (t, t), 0)
    qpos = lax.broadcasted_iota(jnp.int32, (t, t), 1)

    def visible(c):
        return (kpos - qpos) <= (i - c) * t

    qih, wih = [], []
    for hd in range(IDX_HEADS):
        g, e = divmod(hd, LANES // IDX_DIM)
        grp = qi_ref[0, 0, LANES * g:LANES * (g + 1), :]
        qih.append(jnp.where((feat >= IDX_DIM * e) & (feat < IDX_DIM * (e + 1)), grp, jnp.zeros_like(grp)))
        wih.append(wit_ref[0, 0, hd:hd + 1, :])

    def score_chunk(c, _):
        start = pl.multiple_of(c * t, t)
        kic = ki_ref[0, pl.ds(start, t), :]
        isc = jnp.zeros((t, t), F32)
        for hd in range(IDX_HEADS):
            isc = isc + jnp.maximum(_dot(kic, qih[hd]), 0.0) * wih[hd]
        key_ref[c] = jnp.where(visible(c), _sortable(isc), INT_MIN)
        return 0

    lax.fori_loop(0, nc, score_chunk, 0)

    def as_pattern(d):
        return lax.bitcast_convert_type(lax.shift_left(d, 16), F32).astype(BF16)

    def build_digits(shift, nbits, prefix_shift, prefix):
        def body(c, _):
            kc = key_ref[c]
            u = kc ^ INT_MIN
            member = kc != INT_MIN
            if prefix is not None:
                member = member & (lax.shift_right_logical(u, prefix_shift) == prefix)
            d = lax.shift_right_logical(u, shift) & ((1 << nbits) - 1)
            dig_ref[c] = as_pattern(jnp.where(member, d + DIGIT_BIAS, 0))
            return 0
        lax.fori_loop(0, nc, body, 0)

    one, zero = jnp.ones((), BF16), jnp.zeros((), BF16)

    def count_digits(pred):
        def body(c, accs):
            w = jnp.where(pred(dig_ref[c]), one, zero)
            accs = list(accs)
            for r in range(t // 16):
                accs[r % 2] = accs[r % 2] + w[16 * r:16 * (r + 1)]
            return tuple(accs)
        a0, a1 = lax.fori_loop(0, nc, body, (jnp.zeros((16, t), BF16),) * 2)
        return jnp.sum(a0.astype(F32) + a1.astype(F32), axis=0, keepdims=True)

    def search_digit(nbits, n_above):
        def step(b, dg):
            cand = dg | lax.shift_left(jnp.int32(1), nbits - 1 - b)
            cpat = as_pattern(cand + DIGIT_BIAS)
            cnt = n_above + count_digits(lambda x: x >= cpat)
            return jnp.where(cnt >= n_sel, cand, dg)
        dg = lax.fori_loop(0, nbits, step, jnp.zeros((1, t), jnp.int32))
        dpat = as_pattern(dg + DIGIT_BIAS)
        return dg, dpat, n_above + count_digits(lambda x: x > dpat)

    prefix, n_gt, low = None, jnp.zeros((1, t), F32), 32
    for nbits in DIGIT_BITS:
        build_digits(low - nbits, nbits, low, prefix)
        low -= nbits
        dg, dpat, n_gt = search_digit(nbits, n_gt)
        prefix = dg if prefix is None else lax.shift_left(prefix, nbits) | dg
    thr = prefix ^ INT_MIN

    need = n_sel - n_gt
    n_ties = count_digits(lambda x: x == dpat)

    def tie_cut():
        def count(pred):
            def body(c, cnt):
                hit = pred(key_ref[c], kpos + c * t)
                return cnt + jnp.sum(jnp.where(hit, 1.0, 0.0), axis=0, keepdims=True)
            return lax.fori_loop(0, nc, body, jnp.zeros((1, t), F32))

        def cut_step(b, cut):
            cand = cut | lax.shift_left(jnp.int32(1), 10 - b)
            cnt = count(lambda kc, idx: (kc == thr) & (idx < cand))
            return jnp.where(cnt < need, cand, cut)

        nbits = max(1, int(np.ceil(np.log2(k_ref.shape[1]))))
        return lax.fori_loop(11 - nbits, 11, cut_step, jnp.zeros((1, t), jnp.int32))

    take_all = lambda: jnp.full((1, t), k_ref.shape[1], jnp.int32)
    cut = lax.cond(jnp.max(n_ties - need) > 0.0, tie_cut, take_all)

    def bias_chunk(c, _):
        kc = key_ref[c]
        keep = visible(c) & ((kc > thr) | ((kc == thr) & (kpos + c * t <= cut)))
        bias_ref[c] = jnp.where(keep, 0.0, NEG)
        return 0

    lax.fori_loop(0, nc, bias_chunk, 0)

    qh = []
    for hd in range(DSA_HEADS):
        g, e = divmod(hd, 2)
        grp = q_ref[0, 0, LANES * g:LANES * (g + 1), :]
        qh.append(jnp.where((feat >= 64 * e) & (feat < 64 * (e + 1)), grp, jnp.zeros_like(grp)))

    def scores(c):
        start = pl.multiple_of(c * t, t)
        ks = k_ref[0, pl.ds(start, t), :]
        return [ks] * DSA_HEADS, lambda hd, r, s: s + bias_ref[c, r:r + STRIP, :]

    def step(j, carry):
        cmax, stats = carry
        kcs, finish = scores(j)
        return _pipelined_step(st_ref, pt_ref, kcs, qh, finish, cmax, stats, [vt_ref[0, j - 1]] * DSA_HEADS)

    kcs, finish = scores(0)
    init = (_score_pass(st_ref, 0, kcs, qh, finish), tuple(_softmax_init(t, DSA_DIM) for _ in range(DSA_HEADS)))
    cmax, stats = lax.fori_loop(1, nc, step, init)
    stats = _softmax_pass(st_ref, pt_ref, 0, cmax, stats, [vt_ref[0, nc - 1]] * DSA_HEADS)
    for g in range(DSA_HEADS // 2):
        o_ref[0, :, LANES * g:LANES * (g + 1)] = _pair_output(stats[2 * g:2 * g + 2])


def _dsa(qt, k, vt, qit, ki, wit, n_sel):
    b, s, _ = k.shape
    t = ATT_T
    tile = lambda w: pl.BlockSpec((1, t, w), lambda bi, i: (bi, i, 0))
    seq = lambda w: pl.BlockSpec((1, s, w), lambda bi, i: (bi, 0, 0))
    trans = lambda r: pl.BlockSpec((1, 1, r, t), lambda bi, i: (bi, i, 0, 0))
    return pl.pallas_call(
        functools.partial(_dsa_kernel, n_sel=n_sel),
        grid=(b, s // t),
        in_specs=[trans(256), seq(LANES), pl.BlockSpec((1, s // t, DSA_DIM, t), lambda bi, i: (bi, 0, 0, 0)),
                  trans(256), seq(LANES), trans(IDX_HEADS)],
        out_specs=tile(256),
        out_shape=jax.ShapeDtypeStruct((b, s, 256), BF16),
        scratch_shapes=[pltpu.VMEM((s // t, t, t), jnp.int32), pltpu.VMEM((s // t, t, t), F32),
                        pltpu.VMEM((s // t, t, t), BF16),
                        pltpu.VMEM((1, DSA_HEADS, t, t), F32), pltpu.VMEM((DSA_HEADS, t, t), BF16)],
        compiler_params=_params("parallel", "arbitrary"),
        name="dsa",
    )(qt, k, vt, qit, ki, wit)


def _memkv_kernel(m_ref, g_ref, w_ref, k_ref, v_ref):
    hm = _rmsnorm(m_ref[0], g_ref[...]).astype(BF16)
    kv = _dot(hm, w_ref[...])
    half = kv.shape[1] // 2
    k_ref[0] = kv[:, :half].astype(BF16)
    v_ref[0] = kv[:, half:].astype(BF16)


def _memkv(mem, gain, wkv, layer):
    b, ml, d = mem.shape
    n = XA_HEADS * XA_DIM
    return pl.pallas_call(
        _memkv_kernel,
        grid=(b,),
        in_specs=[pl.BlockSpec((1, ml, d), lambda bi: (bi, 0, 0)),
                  _vec_spec(d, layer),
                  pl.BlockSpec((None, d, 2 * n), lambda bi: (layer, 0, 0))],
        out_specs=[pl.BlockSpec((1, ml, n), lambda bi: (bi, 0, 0))] * 2,
        out_shape=[jax.ShapeDtypeStruct((b, ml, n), BF16)] * 2,
        compiler_params=_params("parallel"),
        name="memkv",
    )(mem, gain, wkv)


def _mix_kernel(x_ref, a_ref, b_ref, c_ref, wout_ref, g_ref, wq_ref, kx_ref, vx_ref, wo_ref, o_ref):
    mixed = jnp.concatenate([a_ref[0], b_ref[0], c_ref[0]], axis=-1)
    x1 = x_ref[0] + _dot(mixed, wout_ref[...])
    hq = _rmsnorm(x1, g_ref[...]).astype(BF16)
    q = (_dot(hq, wq_ref[...]) * (XA_DIM ** -0.5 * LOG2E)).astype(BF16)
    outs = []
    for hd in range(XA_HEADS):
        sl = slice(XA_DIM * hd, XA_DIM * (hd + 1))
        s = _dot_nt(q[:, sl], kx_ref[0, :, sl])
        pe = jnp.exp2(s - jnp.max(s, axis=-1, keepdims=True))
        o = _dot(pe.astype(BF16), vx_ref[0, :, sl]) / jnp.sum(pe, axis=-1, keepdims=True)
        outs.append(o.astype(BF16))
    o_ref[0] = x1 + _dot(jnp.concatenate(outs, axis=-1), wo_ref[...])


def _mix_xattn(x, a, bm, c, wout, gain, wq, kx, vx, wo, layer):
    b, s, d = x.shape
    tm = MIX_TM
    n = XA_HEADS * XA_DIM
    ml = kx.shape[1]
    tile = lambda w: pl.BlockSpec((1, tm, w), lambda bi, j: (bi, j, 0))
    full = lambda r, cdim: pl.BlockSpec((None, r, cdim), lambda bi, j: (layer, 0, 0))
    mem = pl.BlockSpec((1, ml, n), lambda bi, j: (bi, 0, 0))
    return pl.pallas_call(
        _mix_kernel,
        grid=(b, s // tm),
        in_specs=[tile(d), tile(a.shape[2]), tile(bm.shape[2]), tile(c.shape[2]), full(d, d),
                  _vec_spec(d, layer), full(d, n), mem, mem, full(n, d)],
        out_specs=tile(d),
        out_shape=jax.ShapeDtypeStruct((b, s, d), F32),
        compiler_params=_params("parallel", "parallel"),
        name="mix_xattn",
    )(x, a, bm, c, wout, gain, wq, kx, vx, wo)


def _rope_tables(s, d):
    half = d // 2
    inv = ROPE_THETA ** (-jnp.arange(half, dtype=F32) / half)
    ang = jnp.arange(s).astype(F32)[:, None] * inv[None, :]
    cos, sin = jnp.cos(ang), jnp.sin(ang)
    reps = LANES // d
    return (jnp.tile(jnp.concatenate([cos, cos], -1), (1, reps)),
            jnp.tile(jnp.concatenate([-sin, sin], -1), (1, reps)))


def _layout_w_in(w_in):
    sizes = (384, 384, 384, FOX_HEADS, MLA_Q_RANK, MLA_KV_RANK, MLA_ROPE,
             DSA_HEADS * DSA_DIM, DSA_DIM, DSA_DIM, IDX_HEADS * IDX_DIM, IDX_DIM, IDX_HEADS)
    offs = np.concatenate([[0], np.cumsum(sizes)])
    fq, fk, fv, ff, mcq, mckv, mkr, dq, dk, dv, dqi, dki, dwi = (
        w_in[:, :, offs[n]:offs[n + 1]] for n in range(len(sizes)))
    zeros = lambda w: jnp.zeros(w_in.shape[:2] + (w,), w_in.dtype)
    pad = lambda a: jnp.concatenate([a, zeros(LANES - a.shape[2])], -1)
    out = jnp.concatenate([
        fq, fk, fv, pad(ff), mcq, mckv,
        jnp.concatenate([mkr, mkr, zeros(64)], -1),
        dq, jnp.concatenate([dk, dk], -1), jnp.concatenate([dv, dv], -1),
        dqi, jnp.concatenate([dki] * 4, -1), pad(dwi)], -1)
    assert out.shape[2] == N_PROJ
    return out.astype(BF16)


def _layout_w_uq(w_uq):
    hw = MLA_NOPE + MLA_ROPE
    cols = []
    for pr in range(MLA_HEADS // 2):
        h0, h1 = 2 * pr, 2 * pr + 1
        cols += [w_uq[:, :, hw * h0:hw * h0 + MLA_NOPE], w_uq[:, :, hw * h1:hw * h1 + MLA_NOPE],
                 w_uq[:, :, hw * h0 + MLA_NOPE:hw * (h0 + 1)], w_uq[:, :, hw * h1 + MLA_NOPE:hw * (h1 + 1)],
                 jnp.zeros(w_uq.shape[:2] + (64,), w_uq.dtype)]
    return jnp.concatenate(cols, -1).astype(BF16)


def _layout_w_ukv(w_ukv):
    hw = MLA_NOPE + MLA_V
    kn = [w_ukv[:, :, hw * hd:hw * hd + MLA_NOPE] for hd in range(MLA_HEADS)]
    vv = [w_ukv[:, :, hw * hd + MLA_NOPE:hw * (hd + 1)] for hd in range(MLA_HEADS)]
    return jnp.concatenate(kn + vv, -1).astype(BF16)


def kernel(x, mem, ffn1_norm, ffn1_wi, ffn1_wo, mix_norm, w_in, b_forget, mla_q_norm, mla_w_uq,
           mla_kv_norm, mla_w_ukv, w_out, xa_norm, mem_norm, xa_wq, xa_wkv, xa_wo,
           ffn2_norm, ffn2_wi, ffn2_wo, final_norm):
    b, s, d = x.shape
    depth = w_in.shape[0]
    assert s % ATT_T == 0 and s % MIX_TM == 0 and (b * s) % FFN_TM == 0 and s <= 2048
    n_sel = min(TOPK_MAX, s // 4)
    assert PROJ_TM == ATT_T and n_sel <= ATT_T

    win = _layout_w_in(w_in)
    wuq = _layout_w_uq(mla_w_uq)
    wukv = _layout_w_ukv(mla_w_ukv)
    rows = lambda v: v.reshape(-1, 1, v.shape[-1])
    bfp = rows(jnp.concatenate([b_forget, jnp.zeros((depth, LANES - FOX_HEADS), F32)], -1))
    ffn1_norm, mix_norm, mla_q_norm, mla_kv_norm, xa_norm, mem_norm, ffn2_norm, final_norm = map(
        rows, (ffn1_norm, mix_norm, mla_q_norm, mla_kv_norm, xa_norm, mem_norm, ffn2_norm, final_norm))
    tabs = _rope_tables(s, IDX_DIM) + _rope_tables(s, DSA_DIM)
    w1i, w1o = ffn1_wi.astype(BF16), ffn1_wo.astype(BF16)
    w2i, w2o = ffn2_wi.astype(BF16), ffn2_wo.astype(BF16)
    wout, wq, wkv, wo = (w.astype(BF16) for w in (w_out, xa_wq, xa_wkv, xa_wo))

    for layer in range(depth):
        x = _ffn(x.reshape(b * s, d), ffn1_norm, w1i, w1o, layer).reshape(b, s, d)
        (fq, fk, fv, cq, ck, mq, mk, mv, dq, dk, dv, qi, ki, wi) = _proj(
            x, mix_norm, win, bfp, mla_q_norm, wuq, mla_kv_norm, wukv, tabs, layer)
        a_out = _attn(fq, fk, fv, cq, ck, kw=LANES, name="fox_attn")
        b_out = _attn(mq, mk, mv, kw=2 * LANES, name="mla_attn")
        c_out = _dsa(dq, dk, dv, qi, ki, wi, n_sel)
        kx, vx = _memkv(mem, mem_norm, wkv, layer)
        x = _mix_xattn(x, a_out, b_out, c_out, wout, xa_norm, wq, kx, vx, wo, layer)
        last = layer == depth - 1
        x = _ffn(x.reshape(b * s, d), ffn2_norm, w2i, w2o, layer,
                 final_gain=final_norm if last else None).reshape(b, s, d)
    return x
```

```python
import functools

import jax
import jax.numpy as jnp
import numpy as np
from jax import lax
from jax.experimental import pallas as pl
from jax.experimental.pallas import tpu as pltpu

F32 = jnp.float32
BF16 = jnp.bfloat16

ROPE_THETA = 10000.0
NORM_EPS = 1e-6
FOX_HEADS, FOX_DIM = 6, 64
MLA_HEADS, MLA_NOPE, MLA_ROPE, MLA_V = 6, 64, 32, 64
MLA_Q_RANK, MLA_KV_RANK = 256, 128
DSA_HEADS, DSA_DIM = 4, 64
IDX_HEADS, IDX_DIM = 8, 32
TOPK_MAX = 256
XA_HEADS, XA_DIM = 4, 128

LANES = 128
VMEM_LIMIT = 56 * 1024 * 1024
NEG = -1e30
LOG2E = 1.4426950408889634
INT_MIN = -(2 ** 31)
DIGIT_BITS = (14, 14, 4)
DIGIT_BIAS = 0x80

O_FQ, O_FK, O_FV, O_FF = 0, 384, 768, 1152
O_MCQ, O_MCKV, O_MKR = 1280, 1536, 1664
O_DQ, O_DK, O_DV, O_DQI, O_DKI, O_DWI = 1792, 2048, 2176, 2304, 2560, 2688
N_PROJ = 2816

FFN_TM, FFN_TF = 1024, 256
PROJ_TM = 256
ATT_T = 256
STRIP = 64
MIX_TM = 512


def _dot(a, b):
    return jnp.dot(a, b, preferred_element_type=F32)


def _dot_nt(a, b):
    return lax.dot_general(a, b, (((1,), (1,)), ((), ())), preferred_element_type=F32)


def _rmsnorm(x, g):
    return x * lax.rsqrt(jnp.mean(x * x, axis=-1, keepdims=True) + NORM_EPS) * g


def _vec_spec(width, layer):
    return pl.BlockSpec((None, 1, width), lambda *_: (layer, 0, 0))


def _params(*sem):
    return pltpu.CompilerParams(dimension_semantics=sem, vmem_limit_bytes=VMEM_LIMIT)


def _ffn_kernel(x_ref, g_ref, wi_ref, wo_ref, fg_ref, o_ref, h_ref, act_ref, *, final_norm):
    f = wo_ref.shape[0]
    h_ref[...] = _rmsnorm(x_ref[...], g_ref[...]).astype(BF16)
    for j in range(f // FFN_TF):
        lo, hi = j * FFN_TF, (j + 1) * FFN_TF
        gate = _dot(h_ref[...], wi_ref[:, lo:hi])
        up = _dot(h_ref[...], wi_ref[:, f + lo:f + hi])
        act_ref[:, lo:hi] = (gate * jax.nn.sigmoid(gate) * up).astype(BF16)
    y = x_ref[...] + 0.5 * _dot(act_ref[...], wo_ref[...])
    if final_norm:
        y = _rmsnorm(y, fg_ref[...])
    o_ref[...] = y


def _ffn(x2d, gain, wi, wo, layer, final_gain=None):
    m, d = x2d.shape
    f = wo.shape[1]
    assert f % FFN_TF == 0
    fg = gain if final_gain is None else final_gain
    resident = lambda r, c: pl.BlockSpec((None, r, c), lambda i: (layer, 0, 0), pipeline_mode=pl.Buffered(1))
    return pl.pallas_call(
        functools.partial(_ffn_kernel, final_norm=final_gain is not None),
        grid=(m // FFN_TM,),
        in_specs=[
            pl.BlockSpec((FFN_TM, d), lambda i: (i, 0)),
            _vec_spec(d, layer),
            resident(d, 2 * f),
            resident(f, d),
            _vec_spec(d, 0 if final_gain is not None else layer),
        ],
        out_specs=pl.BlockSpec((FFN_TM, d), lambda i: (i, 0)),
        out_shape=jax.ShapeDtypeStruct((m, d), F32),
        scratch_shapes=[pltpu.VMEM((FFN_TM, d), BF16), pltpu.VMEM((FFN_TM, f), BF16)],
        compiler_params=_params("parallel"),
        name="ffn",
    )(x2d, gain, wi, wo, fg)


def _rope128(x, cos, sin, half):
    lane = lax.broadcasted_iota(jnp.int32, x.shape, 1)
    first = (lane % (2 * half)) < half
    partner = jnp.where(first, pltpu.roll(x, LANES - half, 1), pltpu.roll(x, half, 1))
    return x * cos + partner * sin


def _proj_kernel(x_ref, g_ref, win_ref, bf_ref, gq_ref, wuq_ref, gkv_ref, wukv_ref,
                 c32_ref, s32_ref, c64_ref, s64_ref,
                 fq_ref, fk_ref, fv_ref, cq_ref, ck_ref,
                 mq_ref, mk_ref, mv_ref,
                 dq_ref, dk_ref, dv_ref, qi_ref, ki_ref, wi_ref,
                 carry_ref):
    j = pl.program_id(1)
    tm = x_ref.shape[1]

    @pl.when(j == 0)
    def _():
        carry_ref[...] = jnp.zeros_like(carry_ref)

    h = _rmsnorm(x_ref[0], g_ref[...]).astype(BF16)
    p = _dot(h, win_ref[...])

    def grp(off, width=LANES):
        return p[:, off:off + width]

    c32, s32 = c32_ref[...], s32_ref[...]
    c64, s64 = c64_ref[...], s64_ref[...]

    fq_ref[0, 0] = (grp(O_FQ, 384) * (FOX_DIM ** -0.5 * LOG2E)).T.astype(BF16)
    fk_ref[0] = grp(O_FK, 384).astype(BF16)
    fv_ref[0, 0] = grp(O_FV, 384).T.astype(BF16)

    z = grp(O_FF) + bf_ref[...]
    logf = jnp.minimum(z, 0.0) - jnp.log(1.0 + jnp.exp(-jnp.abs(z)))
    hi = logf.astype(BF16)
    r1 = logf - hi.astype(F32)
    mid = r1.astype(BF16)
    lo = (r1 - mid.astype(F32)).astype(BF16)
    row = lax.broadcasted_iota(jnp.int32, (tm, tm), 0)
    col = lax.broadcasted_iota(jnp.int32, (tm, tm), 1)
    tri = jnp.where(row >= col, 1.0, 0.0).astype(BF16)
    cum = (_dot(tri, hi) + _dot(tri, mid)) + _dot(tri, lo) + carry_ref[0:1, :]
    carry_ref[0:1, :] = cum[tm - 1:tm, :]
    cum2 = cum * LOG2E
    cq_ref[0] = cum2
    ck_ref[0, 0] = cum2.T[0:8, :]

    qn = _rmsnorm(grp(O_MCQ, MLA_Q_RANK), gq_ref[...]).astype(BF16)
    qm = _dot(qn, wuq_ref[...]) * ((MLA_NOPE + MLA_ROPE) ** -0.5 * LOG2E)
    kvn = _rmsnorm(grp(O_MCKV, MLA_KV_RANK), gkv_ref[...]).astype(BF16)
    kvm = _dot(kvn, wukv_ref[...])
    krt = _rope128(grp(O_MKR), c32, s32, MLA_ROPE // 2).astype(BF16)
    for pr in range(MLA_HEADS // 2):
        mq_ref[0, 0, 256 * pr:256 * pr + 128, :] = qm[:, 256 * pr:256 * pr + 128].T.astype(BF16)
        mq_ref[0, 0, 256 * pr + 128:256 * pr + 256, :] = _rope128(
            qm[:, 256 * pr + 128:256 * pr + 256], c32, s32, MLA_ROPE // 2).T.astype(BF16)
        mk_ref[0, :, 256 * pr:256 * pr + 128] = kvm[:, 128 * pr:128 * pr + 128].astype(BF16)
        mk_ref[0, :, 256 * pr + 128:256 * pr + 256] = krt
    mv_ref[0, 0] = kvm[:, 384:768].T.astype(BF16)

    for gi in range(2):
        dq_ref[0, 0, 128 * gi:128 * gi + 128, :] = (
            _rope128(grp(O_DQ + 128 * gi), c64, s64, DSA_DIM // 2) * (DSA_DIM ** -0.5 * LOG2E)).T.astype(BF16)
        qi_ref[0, 0, 128 * gi:128 * gi + 128, :] = (
            _rope128(grp(O_DQI + 128 * gi), c32, s32, IDX_DIM // 2) * (IDX_DIM ** -0.5)).T.astype(BF16)
    dk_ref[0] = _rope128(grp(O_DK), c64, s64, DSA_DIM // 2).astype(BF16)
    dv_ref[0, 0] = grp(O_DV).T[0:DSA_DIM, :].astype(BF16)
    ki_ref[0] = _rope128(grp(O_DKI), c32, s32, IDX_DIM // 2).astype(BF16)
    wi_ref[0, 0] = (grp(O_DWI) * (IDX_HEADS ** -0.5)).T[0:IDX_HEADS, :]


def _proj(x, gain, win, bfp, gq, wuq, gkv, wukv, tabs, layer):
    b, s, d = x.shape
    tm = PROJ_TM
    nt = s // tm
    tok = lambda w, dt: jax.ShapeDtypeStruct((b, s, w), dt)
    tspec = lambda w: pl.BlockSpec((1, tm, w), lambda bi, j: (bi, j, 0))
    full = lambda shape: pl.BlockSpec((None,) + shape, lambda bi, j: (layer,) + (0,) * len(shape))
    vec = lambda w: _vec_spec(w, layer)
    tab = pl.BlockSpec((tm, LANES), lambda bi, j: (j, 0))
    tr = lambda r, dt: jax.ShapeDtypeStruct((b, nt, r, tm), dt)
    trspec = lambda r: pl.BlockSpec((1, 1, r, tm), lambda bi, j: (bi, j, 0, 0))
    out_shape = [tr(384, BF16), tok(384, BF16), tr(384, BF16), tok(LANES, F32), tr(8, F32),
                 tr(768, BF16), tok(768, BF16), tr(384, BF16),
                 tr(256, BF16), tok(LANES, BF16), tr(DSA_DIM, BF16),
                 tr(256, BF16), tok(LANES, BF16), tr(IDX_HEADS, F32)]
    out_specs = [trspec(384), tspec(384), trspec(384), tspec(LANES), trspec(8),
                 trspec(768), tspec(768), trspec(384),
                 trspec(256), tspec(LANES), trspec(DSA_DIM),
                 trspec(256), tspec(LANES), trspec(IDX_HEADS)]
    return pl.pallas_call(
        _proj_kernel,
        grid=(b, nt),
        in_specs=[tspec(d), vec(d), full((d, N_PROJ)), vec(LANES),
                  vec(MLA_Q_RANK), full((MLA_Q_RANK, 768)), vec(MLA_KV_RANK), full((MLA_KV_RANK, 768)),
                  tab, tab, tab, tab],
        out_specs=out_specs,
        out_shape=out_shape,
        scratch_shapes=[pltpu.VMEM((8, LANES), F32)],
        compiler_params=_params("arbitrary", "arbitrary"),
        name="proj",
    )(x, gain, win, bfp, gq, wuq, gkv, wukv, *tabs)


def _score_pass(st_ref, slot, k_chunks, qh, finish):
    sts = [_dot(kc, q) for kc, q in zip(k_chunks, qh)]
    cmax = []
    for hd, st in enumerate(sts):
        mx = None
        for r in range(0, st.shape[0], STRIP):
            s = finish(hd, r, st[r:r + STRIP])
            st_ref[slot, hd, r:r + STRIP, :] = s
            pm = jnp.max(s, axis=0, keepdims=True)
            mx = pm if mx is None else jnp.maximum(mx, pm)
        cmax.append(mx)
    return tuple(cmax)


def _probabilities(st_ref, pt_ref, hd, m, cm):
    m_new = jnp.maximum(m, cm)
    for r in range(0, st_ref.shape[2], STRIP):
        pt_ref[hd, r:r + STRIP, :] = jnp.exp2(st_ref[0, hd, r:r + STRIP, :] - m_new).astype(BF16)
    return m_new, jnp.exp2(m - m_new)


def _accumulate(pt_ref, pre, stats, vts):
    ones = jnp.ones((16, pt_ref.shape[1]), BF16)
    pvs = [_dot(jnp.concatenate([vt, ones], axis=0), pt_ref[hd]) for hd, vt in enumerate(vts)]
    out = []
    for (m_new, alpha), (_, l, acc), pv in zip(pre, stats, pvs):
        dv = acc.shape[0]
        out.append((m_new, alpha * l + pv[dv:dv + 1], alpha * acc + pv[:dv]))
    return tuple(out)


def _softmax_pass(st_ref, pt_ref, cmax, stats, vts):
    pre = [_probabilities(st_ref, pt_ref, hd, m, cm) for hd, (cm, (m, _, _)) in enumerate(zip(cmax, stats))]
    return _accumulate(pt_ref, pre, stats, vts)


def _pipelined_step(st_ref, pt_ref, k_chunks, qh, finish, cmax, stats, vts):
    pre, nxt = [], []
    for hd, (kc, q) in enumerate(zip(k_chunks, qh)):
        st_new = _dot(kc, q)
        pre.append(_probabilities(st_ref, pt_ref, hd, stats[hd][0], cmax[hd]))
        mx = None
        for r in range(0, st_new.shape[0], STRIP):
            s = finish(hd, r, st_new[r:r + STRIP])
            st_ref[0, hd, r:r + STRIP, :] = s
            pm = jnp.max(s, axis=0, keepdims=True)
            mx = pm if mx is None else jnp.maximum(mx, pm)
        nxt.append(mx)
    return tuple(nxt), _accumulate(pt_ref, pre, stats, vts)


def _softmax_init(t, dv):
    return (jnp.full((1, t), NEG, F32), jnp.zeros((1, t), F32), jnp.zeros((dv, t), F32))


def _pair_output(carry):
    ot = jnp.concatenate([carry[0][2] / carry[0][1], carry[1][2] / carry[1][1]], axis=0)
    return ot.T.astype(BF16)


def _attn_kernel(*refs, n_pairs, kw, has_bias):
    if has_bias:
        q_ref, k_ref, vt_ref, ccol_ref, crow_ref, o_ref, st_ref, pt_ref = refs
    else:
        q_ref, k_ref, vt_ref, o_ref, st_ref, pt_ref = refs
    i = pl.program_id(1)
    t = q_ref.shape[3]
    feat = lax.broadcasted_iota(jnp.int32, (kw, t), 0)
    kpos = lax.broadcasted_iota(jnp.int32, (STRIP, t), 0)
    qpos = lax.broadcasted_iota(jnp.int32, (STRIP, t), 1)

    qh = []
    for pr in range(n_pairs):
        qs = q_ref[0, 0, kw * pr:kw * (pr + 1), :]
        for e in range(2):
            sel = (feat >= 64 * e) & (feat < 64 * e + 64)
            if kw == 256:
                sel = sel | ((feat >= 128 + 32 * e) & (feat < 160 + 32 * e))
            qh.append(jnp.where(sel, qs, jnp.zeros_like(qs)))

    heads = range(2 * n_pairs)

    def chunk(c, diag):
        start = pl.multiple_of(c * t, t)

        def finish(hd, r, s):
            if has_bias:
                s = s + (crow_ref[0, 0, hd:hd + 1, :] - ccol_ref[0, pl.ds(start + r, STRIP), hd:hd + 1])
            if diag:
                s = jnp.where(kpos + r <= qpos, s, NEG)
            return s

        return [k_ref[0, pl.ds(start, t), kw * (hd // 2):kw * (hd // 2 + 1)] for hd in heads], finish

    def values(c):
        return [vt_ref[0, c, 64 * hd:64 * (hd + 1), :] for hd in heads]

    def step(j, carry):
        cmax, stats = carry
        kcs, finish = chunk(j, False)
        return _pipelined_step(st_ref, pt_ref, kcs, qh, finish, cmax, stats,
                               values(jnp.where(j == 0, i, j - 1)))

    kcs, finish = chunk(i, True)
    init = (_score_pass(st_ref, 0, kcs, qh, finish), tuple(_softmax_init(t, 64) for _ in heads))
    cmax, stats = lax.fori_loop(0, i, step, init)
    stats = _softmax_pass(st_ref, pt_ref, cmax, stats, values(jnp.maximum(i - 1, 0)))
    for pr in range(n_pairs):
        o_ref[0, :, LANES * pr:LANES * (pr + 1)] = _pair_output(stats[2 * pr:2 * pr + 2])


def _attn(qt, k, vt, ccol=None, crow=None, *, kw, name):
    b, s, qw = k.shape
    n_pairs = qw // kw
    t = ATT_T
    has_bias = ccol is not None
    in_specs = [pl.BlockSpec((1, 1, qw, t), lambda bi, i: (bi, i, 0, 0)),
                pl.BlockSpec((1, s, qw), lambda bi, i: (bi, 0, 0)),
                pl.BlockSpec((1, s // t, n_pairs * LANES, t), lambda bi, i: (bi, 0, 0, 0))]
    args = [qt, k, vt]
    if has_bias:
        in_specs += [pl.BlockSpec((1, s, LANES), lambda bi, i: (bi, 0, 0)),
                     pl.BlockSpec((1, 1, 8, t), lambda bi, i: (bi, i, 0, 0))]
        args += [ccol, crow]
    return pl.pallas_call(
        functools.partial(_attn_kernel, n_pairs=n_pairs, kw=kw, has_bias=has_bias),
        grid=(b, s // t),
        in_specs=in_specs,
        out_specs=pl.BlockSpec((1, t, n_pairs * LANES), lambda bi, i: (bi, i, 0)),
        out_shape=jax.ShapeDtypeStruct((b, s, n_pairs * LANES), BF16),
        scratch_shapes=[pltpu.VMEM((1, 2 * n_pairs, t, t), F32), pltpu.VMEM((2 * n_pairs, t, t), BF16)],
        compiler_params=_params("parallel", "arbitrary"),
        name=name,
    )(*args)


def _sortable(x):
    bits = lax.bitcast_convert_type(x + 0.0, jnp.int32)
    return jnp.where(bits < 0, bits ^ 0x7FFFFFFF, bits)


def _dsa_kernel(q_ref, k_ref, vt_ref, qi_ref, ki_ref, wit_ref, o_ref, key_ref, bias_ref, dig_ref, st_ref, pt_ref, *, n_sel):
    i = pl.program_id(1)
    t = q_ref.shape[3]
    nc = i + 1
    feat = lax.broadcasted_iota(jnp.int32, (LANES, t), 0)
    kpos = lax.broadcasted_iota(jnp.int32, (t, t), 0)
    qpos = lax.broadcasted_iota(jnp.int32, (t, t), 1)

    def visible(c):
        return (kpos - qpos) <= (i - c) * t

    qih, wih = [], []
    for hd in range(IDX_HEADS):
        g, e = divmod(hd, LANES // IDX_DIM)
        grp = qi_ref[0, 0, LANES * g:LANES * (g + 1), :]
        qih.append(jnp.where((feat >= IDX_DIM * e) & (feat < IDX_DIM * (e + 1)), grp, jnp.zeros_like(grp)))
        wih.append(wit_ref[0, 0, hd:hd + 1, :])

    def score_chunk(c, _):
        start = pl.multiple_of(c * t, t)
        kic = ki_ref[0, pl.ds(start, t), :]
        isc = jnp.zeros((t, t), F32)
        for hd in range(IDX_HEADS):
            isc = isc + jnp.maximum(_dot(kic, qih[hd]), 0.0) * wih[hd]
        key_ref[c] = jnp.where(visible(c), _sortable(isc), INT_MIN)
        return 0

    lax.fori_loop(0, nc, score_chunk, 0)

    def as_pattern(d):
        return lax.bitcast_convert_type(lax.shift_left(d, 16), F32).astype(BF16)

    def build_digits(shift, nbits, prefix_shift, prefix):
        def body(c, _):
            kc = key_ref[c]
            u = kc ^ INT_MIN
            member = kc != INT_MIN
            if prefix is not None:
                member = member & (lax.shift_right_logical(u, prefix_shift) == prefix)
            d = lax.shift_right_logical(u, shift) & ((1 << nbits) - 1)
            dig_ref[c] = as_pattern(jnp.where(member, d + DIGIT_BIAS, 0))
            return 0
        lax.fori_loop(0, nc, body, 0)

    one, zero = jnp.ones((), BF16), jnp.zeros((), BF16)

    def count_digits(pred):
        def body(c, accs):
            w = jnp.where(pred(dig_ref[c]), one, zero)
            accs = list(accs)
            for r in range(t // 16):
                accs[r % 2] = accs[r % 2] + w[16 * r:16 * (r + 1)]
            return tuple(accs)
        a0, a1 = lax.fori_loop(0, nc, body, (jnp.zeros((16, t), BF16),) * 2)
        return jnp.sum(a0.astype(F32) + a1.astype(F32), axis=0, keepdims=True)

    def search_digit(nbits, n_above):
        def step(b, dg):
            cand = dg | lax.shift_left(jnp.int32(1), nbits - 1 - b)
            cpat = as_pattern(cand + DIGIT_BIAS)
            cnt = n_above + count_digits(lambda x: x >= cpat)
            return jnp.where(cnt >= n_sel, cand, dg)
        dg = lax.fori_loop(0, nbits, step, jnp.zeros((1, t), jnp.int32))
        dpat = as_pattern(dg + DIGIT_BIAS)
        return dg, dpat, n_above + count_digits(lambda x: x > dpat)

    prefix, n_gt, low = None, jnp.zeros((1, t), F32), 32
    for nbits in DIGIT_BITS[:-1]:
        build_digits(low - nbits, nbits, low, prefix)
        low -= nbits
        dg, dpat, n_gt = search_digit(nbits, n_gt)
        prefix = dg if prefix is None else lax.shift_left(prefix, nbits) | dg
    n_bucket = count_digits(lambda x: x == dpat)

    def tie_cut(thr, need):
        def count(pred):
            def body(c, cnt):
                hit = pred(key_ref[c], kpos + c * t)
                return cnt + jnp.sum(jnp.where(hit, 1.0, 0.0), axis=0, keepdims=True)
            return lax.fori_loop(0, nc, body, jnp.zeros((1, t), F32))

        def cut_step(b, cut):
            cand = cut | lax.shift_left(jnp.int32(1), 10 - b)
            cnt = count(lambda kc, idx: (kc == thr) & (idx < cand))
            return jnp.where(cnt < need, cand, cut)

        nbits = max(1, int(np.ceil(np.log2(k_ref.shape[1]))))
        return lax.fori_loop(11 - nbits, 11, cut_step, jnp.zeros((1, t), jnp.int32))

    take_all = lambda: jnp.full((1, t), k_ref.shape[1], jnp.int32)
    last = DIGIT_BITS[-1]

    def split_bucket():
        build_digits(0, last, low, prefix)
        dg, dpat, n_above = search_digit(last, n_gt)
        thr = (lax.shift_left(prefix, last) | dg) ^ INT_MIN
        need = n_sel - n_above
        n_ties = count_digits(lambda x: x == dpat)
        return thr, lax.cond(jnp.max(n_ties - need) > 0.0, lambda: tie_cut(thr, need), take_all)

    def whole_bucket():
        return lax.shift_left(prefix, last) ^ INT_MIN, take_all()

    thr, cut = lax.cond(jnp.max(n_gt + n_bucket) > n_sel, split_bucket, whole_bucket)

    def bias_chunk(c, _):
        kc = key_ref[c]
        keep = visible(c) & ((kc > thr) | ((kc == thr) & (kpos + c * t <= cut)))
        bias_ref[c] = jnp.where(keep, 0.0, NEG)
        return 0

    lax.fori_loop(0, nc, bias_chunk, 0)

    qh = []
    for hd in range(DSA_HEADS):
        g, e = divmod(hd, 2)
        grp = q_ref[0, 0, LANES * g:LANES * (g + 1), :]
        qh.append(jnp.where((feat >= 64 * e) & (feat < 64 * (e + 1)), grp, jnp.zeros_like(grp)))

    def scores(c):
        start = pl.multiple_of(c * t, t)
        ks = k_ref[0, pl.ds(start, t), :]
        return [ks] * DSA_HEADS, lambda hd, r, s: s + bias_ref[c, r:r + STRIP, :]

    def step(j, carry):
        cmax, stats = carry
        kcs, finish = scores(j)
        return _pipelined_step(st_ref, pt_ref, kcs, qh, finish, cmax, stats, [vt_ref[0, j - 1]] * DSA_HEADS)

    kcs, finish = scores(0)
    init = (_score_pass(st_ref, 0, kcs, qh, finish), tuple(_softmax_init(t, DSA_DIM) for _ in range(DSA_HEADS)))
    cmax, stats = lax.fori_loop(1, nc, step, init)
    stats = _softmax_pass(st_ref, pt_ref, cmax, stats, [vt_ref[0, nc - 1]] * DSA_HEADS)
    for g in range(DSA_HEADS // 2):
        o_ref[0, :, LANES * g:LANES * (g + 1)] = _pair_output(stats[2 * g:2 * g + 2])


def _dsa(qt, k, vt, qit, ki, wit, n_sel):
    b, s, _ = k.shape
    t = ATT_T
    tile = lambda w: pl.BlockSpec((1, t, w), lambda bi, i: (bi, i, 0))
    seq = lambda w: pl.BlockSpec((1, s, w), lambda bi, i: (bi, 0, 0))
    trans = lambda r: pl.BlockSpec((1, 1, r, t), lambda bi, i: (bi, i, 0, 0))
    return pl.pallas_call(
        functools.partial(_dsa_kernel, n_sel=n_sel),
        grid=(b, s // t),
        in_specs=[trans(256), seq(LANES), pl.BlockSpec((1, s // t, DSA_DIM, t), lambda bi, i: (bi, 0, 0, 0)),
                  trans(256), seq(LANES), trans(IDX_HEADS)],
        out_specs=tile(256),
        out_shape=jax.ShapeDtypeStruct((b, s, 256), BF16),
        scratch_shapes=[pltpu.VMEM((s // t, t, t), jnp.int32), pltpu.VMEM((s // t, t, t), F32),
                        pltpu.VMEM((s // t, t, t), BF16),
                        pltpu.VMEM((1, DSA_HEADS, t, t), F32), pltpu.VMEM((DSA_HEADS, t, t), BF16)],
        compiler_params=_params("parallel", "arbitrary"),
        name="dsa",
    )(qt, k, vt, qit, ki, wit)


def _memkv_kernel(m_ref, g_ref, w_ref, k_ref, v_ref):
    hm = _rmsnorm(m_ref[0], g_ref[...]).astype(BF16)
    kv = _dot(hm, w_ref[...])
    half = kv.shape[1] // 2
    k_ref[0] = kv[:, :half].astype(BF16)
    v_ref[0] = kv[:, half:].astype(BF16)


def _memkv(mem, gain, wkv, layer):
    b, ml, d = mem.shape
    n = XA_HEADS * XA_DIM
    return pl.pallas_call(
        _memkv_kernel,
        grid=(b,),
        in_specs=[pl.BlockSpec((1, ml, d), lambda bi: (bi, 0, 0)),
                  _vec_spec(d, layer),
                  pl.BlockSpec((None, d, 2 * n), lambda bi: (layer, 0, 0))],
        out_specs=[pl.BlockSpec((1, ml, n), lambda bi: (bi, 0, 0))] * 2,
        out_shape=[jax.ShapeDtypeStruct((b, ml, n), BF16)] * 2,
        compiler_params=_params("parallel"),
        name="memkv",
    )(mem, gain, wkv)


def _mix_kernel(x_ref, a_ref, b_ref, c_ref, wout_ref, g_ref, wq_ref, kx_ref, vx_ref, wo_ref, o_ref):
    mixed = jnp.concatenate([a_ref[0], b_ref[0], c_ref[0]], axis=-1)
    x1 = x_ref[0] + _dot(mixed, wout_ref[...])
    hq = _rmsnorm(x1, g_ref[...]).astype(BF16)
    q = (_dot(hq, wq_ref[...]) * (XA_DIM ** -0.5 * LOG2E)).astype(BF16)
    outs = []
    for hd in range(XA_HEADS):
        sl = slice(XA_DIM * hd, XA_DIM * (hd + 1))
        s = _dot_nt(q[:, sl], kx_ref[0, :, sl])
        pe = jnp.exp2(s - jnp.max(s, axis=-1, keepdims=True))
        o = _dot(pe.astype(BF16), vx_ref[0, :, sl]) / jnp.sum(pe, axis=-1, keepdims=True)
        outs.append(o.astype(BF16))
    o_ref[0] = x1 + _dot(jnp.concatenate(outs, axis=-1), wo_ref[...])


def _mix_xattn(x, a, bm, c, wout, gain, wq, kx, vx, wo, layer):
    b, s, d = x.shape
    tm = MIX_TM
    n = XA_HEADS * XA_DIM
    ml = kx.shape[1]
    tile = lambda w: pl.BlockSpec((1, tm, w), lambda bi, j: (bi, j, 0))
    full = lambda r, cdim: pl.BlockSpec((None, r, cdim), lambda bi, j: (layer, 0, 0))
    mem = pl.BlockSpec((1, ml, n), lambda bi, j: (bi, 0, 0))
    return pl.pallas_call(
        _mix_kernel,
        grid=(b, s // tm),
        in_specs=[tile(d), tile(a.shape[2]), tile(bm.shape[2]), tile(c.shape[2]), full(d, d),
                  _vec_spec(d, layer), full(d, n), mem, mem, full(n, d)],
        out_specs=tile(d),
        out_shape=jax.ShapeDtypeStruct((b, s, d), F32),
        compiler_params=_params("parallel", "parallel"),
        name="mix_xattn",
    )(x, a, bm, c, wout, gain, wq, kx, vx, wo)


def _rope_tables(s, d):
    half = d // 2
    inv = ROPE_THETA ** (-jnp.arange(half, dtype=F32) / half)
    ang = jnp.arange(s).astype(F32)[:, None] * inv[None, :]
    cos, sin = jnp.cos(ang), jnp.sin(ang)
    reps = LANES // d
    return (jnp.tile(jnp.concatenate([cos, cos], -1), (1, reps)),
            jnp.tile(jnp.concatenate([-sin, sin], -1), (1, reps)))


def _layout_w_in(w_in):
    sizes = (384, 384, 384, FOX_HEADS, MLA_Q_RANK, MLA_KV_RANK, MLA_ROPE,
             DSA_HEADS * DSA_DIM, DSA_DIM, DSA_DIM, IDX_HEADS * IDX_DIM, IDX_DIM, IDX_HEADS)
    offs = np.concatenate([[0], np.cumsum(sizes)])
    fq, fk, fv, ff, mcq, mckv, mkr, dq, dk, dv, dqi, dki, dwi = (
        w_in[:, :, offs[n]:offs[n + 1]] for n in range(len(sizes)))
    zeros = lambda w: jnp.zeros(w_in.shape[:2] + (w,), w_in.dtype)
    pad = lambda a: jnp.concatenate([a, zeros(LANES - a.shape[2])], -1)
    out = jnp.concatenate([
        fq, fk, fv, pad(ff), mcq, mckv,
        jnp.concatenate([mkr, mkr, zeros(64)], -1),
        dq, jnp.concatenate([dk, dk], -1), jnp.concatenate([dv, dv], -1),
        dqi, jnp.concatenate([dki] * 4, -1), pad(dwi)], -1)
    assert out.shape[2] == N_PROJ
    return out.astype(BF16)


def _layout_w_uq(w_uq):
    hw = MLA_NOPE + MLA_ROPE
    cols = []
    for pr in range(MLA_HEADS // 2):
        h0, h1 = 2 * pr, 2 * pr + 1
        cols += [w_uq[:, :, hw * h0:hw * h0 + MLA_NOPE], w_uq[:, :, hw * h1:hw * h1 + MLA_NOPE],
                 w_uq[:, :, hw * h0 + MLA_NOPE:hw * (h0 + 1)], w_uq[:, :, hw * h1 + MLA_NOPE:hw * (h1 + 1)],
                 jnp.zeros(w_uq.shape[:2] + (64,), w_uq.dtype)]
    return jnp.concatenate(cols, -1).astype(BF16)


def _layout_w_ukv(w_ukv):
    hw = MLA_NOPE + MLA_V
    kn = [w_ukv[:, :, hw * hd:hw * hd + MLA_NOPE] for hd in range(MLA_HEADS)]
    vv = [w_ukv[:, :, hw * hd + MLA_NOPE:hw * (hd + 1)] for hd in range(MLA_HEADS)]
    return jnp.concatenate(kn + vv, -1).astype(BF16)


def kernel(x, mem, ffn1_norm, ffn1_wi, ffn1_wo, mix_norm, w_in, b_forget, mla_q_norm, mla_w_uq,
           mla_kv_norm, mla_w_ukv, w_out, xa_norm, mem_norm, xa_wq, xa_wkv, xa_wo,
           ffn2_norm, ffn2_wi, ffn2_wo, final_norm):
    b, s, d = x.shape
    depth = w_in.shape[0]
    assert s % ATT_T == 0 and s % MIX_TM == 0 and (b * s) % FFN_TM == 0 and s <= 2048
    n_sel = min(TOPK_MAX, s // 4)
    assert PROJ_TM == ATT_T and n_sel <= ATT_T

    win = _layout_w_in(w_in)
    wuq = _layout_w_uq(mla_w_uq)
    wukv = _layout_w_ukv(mla_w_ukv)
    rows = lambda v: v.reshape(-1, 1, v.shape[-1])
    bfp = rows(jnp.concatenate([b_forget, jnp.zeros((depth, LANES - FOX_HEADS), F32)], -1))
    ffn1_norm, mix_norm, mla_q_norm, mla_kv_norm, xa_norm, mem_norm, ffn2_norm, final_norm = map(
        rows, (ffn1_norm, mix_norm, mla_q_norm, mla_kv_norm, xa_norm, mem_norm, ffn2_norm, final_norm))
    tabs = _rope_tables(s, IDX_DIM) + _rope_tables(s, DSA_DIM)
    w1i, w1o = ffn1_wi.astype(BF16), ffn1_wo.astype(BF16)
    w2i, w2o = ffn2_wi.astype(BF16), ffn2_wo.astype(BF16)
    wout, wq, wkv, wo = (w.astype(BF16) for w in (w_out, xa_wq, xa_wkv, xa_wo))

    for layer in range(depth):
        x = _ffn(x.reshape(b * s, d), ffn1_norm, w1i, w1o, layer).reshape(b, s, d)
        (fq, fk, fv, cq, ck, mq, mk, mv, dq, dk, dv, qi, ki, wi) = _proj(
            x, mix_norm, win, bfp, mla_q_norm, wuq, mla_kv_norm, wukv, tabs, layer)
        a_out = _attn(fq, fk, fv, cq, ck, kw=LANES, name="fox_attn")
        b_out = _attn(mq, mk, mv, kw=2 * LANES, name="mla_attn")
        c_out = _dsa(dq, dk, dv, qi, ki, wi, n_sel)
        kx, vx = _memkv(mem, mem_norm, wkv, layer)
        x = _mix_xattn(x, a_out, b_out, c_out, wout, xa_norm, wq, kx, vx, wo, layer)
        last = layer == depth - 1
        x = _ffn(x.reshape(b * s, d), ffn2_norm, w2i, w2o, layer,
                 final_gain=final_norm if last else None).reshape(b, s, d)
    return x
```

```python
import functools

import jax
import jax.numpy as jnp
import numpy as np
from jax import lax
from jax.experimental import pallas as pl
from jax.experimental.pallas import tpu as pltpu

F32 = jnp.float32
BF16 = jnp.bfloat16

ROPE_THETA = 10000.0
NORM_EPS = 1e-6
FOX_HEADS, FOX_DIM = 6, 64
MLA_HEADS, MLA_NOPE, MLA_ROPE, MLA_V = 6, 64, 32, 64
MLA_Q_RANK, MLA_KV_RANK = 256, 128
DSA_HEADS, DSA_DIM = 4, 64
IDX_HEADS, IDX_DIM = 8, 32
TOPK_MAX = 256
XA_HEADS, XA_DIM = 4, 128

LANES = 128
VMEM_LIMIT = 56 * 1024 * 1024
NEG = -1e30
LOG2E = 1.4426950408889634
INT_MIN = -(2 ** 31)
DIGIT_BITS = (14, 14, 4)
DIGIT_BIAS = 0x80

O_FQ, O_FK, O_FV, O_FF = 0, 384, 768, 1152
O_MCQ, O_MCKV, O_MKR = 1280, 1536, 1664
O_DQ, O_DK, O_DV, O_DQI, O_DKI, O_DWI = 1792, 2048, 2176, 2304, 2560, 2688
N_PROJ = 2816

FFN_TM, FFN_TF = 1024, 256
PROJ_TM = 256
ATT_T = 256
STRIP = 128
MIX_TM = 512


def _dot(a, b):
    return jnp.dot(a, b, preferred_element_type=F32)


def _dot_nt(a, b):
    return lax.dot_general(a, b, (((1,), (1,)), ((), ())), preferred_element_type=F32)


def _rmsnorm(x, g):
    return x * lax.rsqrt(jnp.mean(x * x, axis=-1, keepdims=True) + NORM_EPS) * g


def _vec_spec(width, layer):
    return pl.BlockSpec((None, 1, width), lambda *_: (layer, 0, 0))


def _params(*sem):
    return pltpu.CompilerParams(dimension_semantics=sem, vmem_limit_bytes=VMEM_LIMIT)


def _ffn_kernel(x_ref, g_ref, wi_ref, wo_ref, fg_ref, o_ref, h_ref, act_ref, *, final_norm):
    f = wo_ref.shape[0]
    h_ref[...] = _rmsnorm(x_ref[...], g_ref[...]).astype(BF16)
    for j in range(f // FFN_TF):
        lo, hi = j * FFN_TF, (j + 1) * FFN_TF
        gate = _dot(h_ref[...], wi_ref[:, lo:hi])
        up = _dot(h_ref[...], wi_ref[:, f + lo:f + hi])
        act_ref[:, lo:hi] = (gate * jax.nn.sigmoid(gate) * up).astype(BF16)
    y = x_ref[...] + 0.5 * _dot(act_ref[...], wo_ref[...])
    if final_norm:
        y = _rmsnorm(y, fg_ref[...])
    o_ref[...] = y


def _ffn(x2d, gain, wi, wo, layer, final_gain=None):
    m, d = x2d.shape
    f = wo.shape[1]
    assert f % FFN_TF == 0
    fg = gain if final_gain is None else final_gain
    resident = lambda r, c: pl.BlockSpec((None, r, c), lambda i: (layer, 0, 0), pipeline_mode=pl.Buffered(1))
    return pl.pallas_call(
        functools.partial(_ffn_kernel, final_norm=final_gain is not None),
        grid=(m // FFN_TM,),
        in_specs=[
            pl.BlockSpec((FFN_TM, d), lambda i: (i, 0)),
            _vec_spec(d, layer),
            resident(d, 2 * f),
            resident(f, d),
            _vec_spec(d, 0 if final_gain is not None else layer),
        ],
        out_specs=pl.BlockSpec((FFN_TM, d), lambda i: (i, 0)),
        out_shape=jax.ShapeDtypeStruct((m, d), F32),
        scratch_shapes=[pltpu.VMEM((FFN_TM, d), BF16), pltpu.VMEM((FFN_TM, f), BF16)],
        compiler_params=_params("parallel"),
        name="ffn",
    )(x2d, gain, wi, wo, fg)


def _rope128(x, cos, sin, half):
    lane = lax.broadcasted_iota(jnp.int32, x.shape, 1)
    first = (lane % (2 * half)) < half
    partner = jnp.where(first, pltpu.roll(x, LANES - half, 1), pltpu.roll(x, half, 1))
    return x * cos + partner * sin


def _proj_kernel(x_ref, g_ref, win_ref, bf_ref, gq_ref, wuq_ref, gkv_ref, wukv_ref,
                 c32_ref, s32_ref, c64_ref, s64_ref,
                 fq_ref, fk_ref, fv_ref, cq_ref, ck_ref,
                 mq_ref, mk_ref, mv_ref,
                 dq_ref, dk_ref, dv_ref, qi_ref, ki_ref, wi_ref,
                 carry_ref):
    j = pl.program_id(1)
    tm = x_ref.shape[1]

    @pl.when(j == 0)
    def _():
        carry_ref[...] = jnp.zeros_like(carry_ref)

    h = _rmsnorm(x_ref[0], g_ref[...]).astype(BF16)
    p = _dot(h, win_ref[...])

    def grp(off, width=LANES):
        return p[:, off:off + width]

    c32, s32 = c32_ref[...], s32_ref[...]
    c64, s64 = c64_ref[...], s64_ref[...]

    fq_ref[0, 0] = (grp(O_FQ, 384) * (FOX_DIM ** -0.5 * LOG2E)).T.astype(BF16)
    fk_ref[0] = grp(O_FK, 384).astype(BF16)
    fv_ref[0, 0] = grp(O_FV, 384).T.astype(BF16)

    z = grp(O_FF) + bf_ref[...]
    logf = jnp.minimum(z, 0.0) - jnp.log(1.0 + jnp.exp(-jnp.abs(z)))
    hi = logf.astype(BF16)
    r1 = logf - hi.astype(F32)
    mid = r1.astype(BF16)
    lo = (r1 - mid.astype(F32)).astype(BF16)
    row = lax.broadcasted_iota(jnp.int32, (tm, tm), 0)
    col = lax.broadcasted_iota(jnp.int32, (tm, tm), 1)
    tri = jnp.where(row >= col, 1.0, 0.0).astype(BF16)
    cum = (_dot(tri, hi) + _dot(tri, mid)) + _dot(tri, lo) + carry_ref[0:1, :]
    carry_ref[0:1, :] = cum[tm - 1:tm, :]
    cum2 = cum * LOG2E
    cq_ref[0] = cum2
    ck_ref[0, 0] = cum2.T[0:8, :]

    qn = _rmsnorm(grp(O_MCQ, MLA_Q_RANK), gq_ref[...]).astype(BF16)
    qm = _dot(qn, wuq_ref[...]) * ((MLA_NOPE + MLA_ROPE) ** -0.5 * LOG2E)
    kvn = _rmsnorm(grp(O_MCKV, MLA_KV_RANK), gkv_ref[...]).astype(BF16)
    kvm = _dot(kvn, wukv_ref[...])
    krt = _rope128(grp(O_MKR), c32, s32, MLA_ROPE // 2).astype(BF16)
    for pr in range(MLA_HEADS // 2):
        mq_ref[0, 0, 256 * pr:256 * pr + 128, :] = qm[:, 256 * pr:256 * pr + 128].T.astype(BF16)
        mq_ref[0, 0, 256 * pr + 128:256 * pr + 256, :] = _rope128(
            qm[:, 256 * pr + 128:256 * pr + 256], c32, s32, MLA_ROPE // 2).T.astype(BF16)
        mk_ref[0, :, 256 * pr:256 * pr + 128] = kvm[:, 128 * pr:128 * pr + 128].astype(BF16)
        mk_ref[0, :, 256 * pr + 128:256 * pr + 256] = krt
    mv_ref[0, 0] = kvm[:, 384:768].T.astype(BF16)

    for gi in range(2):
        dq_ref[0, 0, 128 * gi:128 * gi + 128, :] = (
            _rope128(grp(O_DQ + 128 * gi), c64, s64, DSA_DIM // 2) * (DSA_DIM ** -0.5 * LOG2E)).T.astype(BF16)
        qi_ref[0, 0, 128 * gi:128 * gi + 128, :] = (
            _rope128(grp(O_DQI + 128 * gi), c32, s32, IDX_DIM // 2) * (IDX_DIM ** -0.5)).T.astype(BF16)
    dk_ref[0] = _rope128(grp(O_DK), c64, s64, DSA_DIM // 2).astype(BF16)
    dv_ref[0, 0] = grp(O_DV).T[0:DSA_DIM, :].astype(BF16)
    ki_ref[0] = _rope128(grp(O_DKI), c32, s32, IDX_DIM // 2).astype(BF16)
    wi_ref[0, 0] = (grp(O_DWI) * (IDX_HEADS ** -0.5)).T[0:IDX_HEADS, :]


def _proj(x, gain, win, bfp, gq, wuq, gkv, wukv, tabs, layer):
    b, s, d = x.shape
    tm = PROJ_TM
    nt = s // tm
    tok = lambda w, dt: jax.ShapeDtypeStruct((b, s, w), dt)
    tspec = lambda w: pl.BlockSpec((1, tm, w), lambda bi, j: (bi, j, 0))
    full = lambda shape: pl.BlockSpec((None,) + shape, lambda bi, j: (layer,) + (0,) * len(shape))
    vec = lambda w: _vec_spec(w, layer)
    tab = pl.BlockSpec((tm, LANES), lambda bi, j: (j, 0))
    tr = lambda r, dt: jax.ShapeDtypeStruct((b, nt, r, tm), dt)
    trspec = lambda r: pl.BlockSpec((1, 1, r, tm), lambda bi, j: (bi, j, 0, 0))
    out_shape = [tr(384, BF16), tok(384, BF16), tr(384, BF16), tok(LANES, F32), tr(8, F32),
                 tr(768, BF16), tok(768, BF16), tr(384, BF16),
                 tr(256, BF16), tok(LANES, BF16), tr(DSA_DIM, BF16),
                 tr(256, BF16), tok(LANES, BF16), tr(IDX_HEADS, F32)]
    out_specs = [trspec(384), tspec(384), trspec(384), tspec(LANES), trspec(8),
                 trspec(768), tspec(768), trspec(384),
                 trspec(256), tspec(LANES), trspec(DSA_DIM),
                 trspec(256), tspec(LANES), trspec(IDX_HEADS)]
    return pl.pallas_call(
        _proj_kernel,
        grid=(b, nt),
        in_specs=[tspec(d), vec(d), full((d, N_PROJ)), vec(LANES),
                  vec(MLA_Q_RANK), full((MLA_Q_RANK, 768)), vec(MLA_KV_RANK), full((MLA_KV_RANK, 768)),
                  tab, tab, tab, tab],
        out_specs=out_specs,
        out_shape=out_shape,
        scratch_shapes=[pltpu.VMEM((8, LANES), F32)],
        compiler_params=_params("arbitrary", "arbitrary"),
        name="proj",
    )(x, gain, win, bfp, gq, wuq, gkv, wukv, *tabs)


def _score_pass(st_ref, slot, k_chunks, qh, finish):
    sts = [_dot(kc, q) for kc, q in zip(k_chunks, qh)]
    cmax = []
    for hd, st in enumerate(sts):
        mx = None
        for r in range(0, st.shape[0], STRIP):
            s = finish(hd, r, st[r:r + STRIP])
            st_ref[slot, hd, r:r + STRIP, :] = s
            pm = jnp.max(s, axis=0, keepdims=True)
            mx = pm if mx is None else jnp.maximum(mx, pm)
        cmax.append(mx)
    return tuple(cmax)


def _probabilities(st_ref, pt_ref, hd, m, cm):
    m_new = jnp.maximum(m, cm)
    for r in range(0, st_ref.shape[2], STRIP):
        pt_ref[hd, r:r + STRIP, :] = jnp.exp2(st_ref[0, hd, r:r + STRIP, :] - m_new).astype(BF16)
    return m_new, jnp.exp2(m - m_new)


def _accumulate(pt_ref, pre, stats, vts):
    ones = jnp.ones((16, pt_ref.shape[1]), BF16)
    pvs = [_dot(jnp.concatenate([vt, ones], axis=0), pt_ref[hd]) for hd, vt in enumerate(vts)]
    out = []
    for (m_new, alpha), (_, l, acc), pv in zip(pre, stats, pvs):
        dv = acc.shape[0]
        out.append((m_new, alpha * l + pv[dv:dv + 1], alpha * acc + pv[:dv]))
    return tuple(out)


def _softmax_pass(st_ref, pt_ref, cmax, stats, vts):
    pre = [_probabilities(st_ref, pt_ref, hd, m, cm) for hd, (cm, (m, _, _)) in enumerate(zip(cmax, stats))]
    return _accumulate(pt_ref, pre, stats, vts)


def _pipelined_step(st_ref, pt_ref, k_chunks, qh, finish, cmax, stats, vts):
    pre, nxt = [], []
    for hd, (kc, q) in enumerate(zip(k_chunks, qh)):
        st_new = _dot(kc, q)
        pre.append(_probabilities(st_ref, pt_ref, hd, stats[hd][0], cmax[hd]))
        mx = None
        for r in range(0, st_new.shape[0], STRIP):
            s = finish(hd, r, st_new[r:r + STRIP])
            st_ref[0, hd, r:r + STRIP, :] = s
            pm = jnp.max(s, axis=0, keepdims=True)
            mx = pm if mx is None else jnp.maximum(mx, pm)
        nxt.append(mx)
    return tuple(nxt), _accumulate(pt_ref, pre, stats, vts)


def _softmax_init(t, dv):
    return (jnp.full((1, t), NEG, F32), jnp.zeros((1, t), F32), jnp.zeros((dv, t), F32))


def _pair_output(carry):
    ot = jnp.concatenate([carry[0][2] / carry[0][1], carry[1][2] / carry[1][1]], axis=0)
    return ot.T.astype(BF16)


def _attn_kernel(*refs, n_pairs, kw, has_bias):
    if has_bias:
        q_ref, k_ref, vt_ref, ccol_ref, crow_ref, o_ref, st_ref, pt_ref = refs
    else:
        q_ref, k_ref, vt_ref, o_ref, st_ref, pt_ref = refs
    i = pl.program_id(1)
    t = q_ref.shape[3]
    feat = lax.broadcasted_iota(jnp.int32, (kw, t), 0)
    kpos = lax.broadcasted_iota(jnp.int32, (STRIP, t), 0)
    qpos = lax.broadcasted_iota(jnp.int32, (STRIP, t), 1)

    qh = []
    for pr in range(n_pairs):
        qs = q_ref[0, 0, kw * pr:kw * (pr + 1), :]
        for e in range(2):
            sel = (feat >= 64 * e) & (feat < 64 * e + 64)
            if kw == 256:
                sel = sel | ((feat >= 128 + 32 * e) & (feat < 160 + 32 * e))
            qh.append(jnp.where(sel, qs, jnp.zeros_like(qs)))

    heads = range(2 * n_pairs)

    def chunk(c, diag):
        start = pl.multiple_of(c * t, t)

        def finish(hd, r, s):
            if has_bias:
                s = s + (crow_ref[0, 0, hd:hd + 1, :] - ccol_ref[0, pl.ds(start + r, STRIP), hd:hd + 1])
            if diag:
                s = jnp.where(kpos + r <= qpos, s, NEG)
            return s

        return [k_ref[0, pl.ds(start, t), kw * (hd // 2):kw * (hd // 2 + 1)] for hd in heads], finish

    def values(c):
        return [vt_ref[0, c, 64 * hd:64 * (hd + 1), :] for hd in heads]

    def step(j, carry):
        cmax, stats = carry
        kcs, finish = chunk(j, False)
        return _pipelined_step(st_ref, pt_ref, kcs, qh, finish, cmax, stats,
                               values(jnp.where(j == 0, i, j - 1)))

    kcs, finish = chunk(i, True)
    init = (_score_pass(st_ref, 0, kcs, qh, finish), tuple(_softmax_init(t, 64) for _ in heads))
    cmax, stats = lax.fori_loop(0, i, step, init)
    stats = _softmax_pass(st_ref, pt_ref, cmax, stats, values(jnp.maximum(i - 1, 0)))
    for pr in range(n_pairs):
        o_ref[0, :, LANES * pr:LANES * (pr + 1)] = _pair_output(stats[2 * pr:2 * pr + 2])


def _attn(qt, k, vt, ccol=None, crow=None, *, kw, name):
    b, s, qw = k.shape
    n_pairs = qw // kw
    t = ATT_T
    has_bias = ccol is not None
    in_specs = [pl.BlockSpec((1, 1, qw, t), lambda bi, i: (bi, i, 0, 0)),
                pl.BlockSpec((1, s, qw), lambda bi, i: (bi, 0, 0)),
                pl.BlockSpec((1, s // t, n_pairs * LANES, t), lambda bi, i: (bi, 0, 0, 0))]
    args = [qt, k, vt]
    if has_bias:
        in_specs += [pl.BlockSpec((1, s, LANES), lambda bi, i: (bi, 0, 0)),
                     pl.BlockSpec((1, 1, 8, t), lambda bi, i: (bi, i, 0, 0))]
        args += [ccol, crow]
    return pl.pallas_call(
        functools.partial(_attn_kernel, n_pairs=n_pairs, kw=kw, has_bias=has_bias),
        grid=(b, s // t),
        in_specs=in_specs,
        out_specs=pl.BlockSpec((1, t, n_pairs * LANES), lambda bi, i: (bi, i, 0)),
        out_shape=jax.ShapeDtypeStruct((b, s, n_pairs * LANES), BF16),
        scratch_shapes=[pltpu.VMEM((1, 2 * n_pairs, t, t), F32), pltpu.VMEM((2 * n_pairs, t, t), BF16)],
        compiler_params=_params("parallel", "arbitrary"),
        name=name,
    )(*args)


def _sortable(x):
    bits = lax.bitcast_convert_type(x + 0.0, jnp.int32)
    return jnp.where(bits < 0, bits ^ 0x7FFFFFFF, bits)


def _dsa_kernel(q_ref, k_ref, vt_ref, qi_ref, ki_ref, wit_ref, o_ref, key_ref, bias_ref, dig_ref, st_ref, pt_ref, *, n_sel):
    i = pl.program_id(1)
    t = q_ref.shape[3]
    nc = i + 1
    feat = lax.broadcasted_iota(jnp.int32, (LANES, t), 0)
    kpos = lax.broadcasted_iota(jnp.int32, (t, t), 0)
    qpos = lax.broadcasted_iota(jnp.int32, (t, t), 1)

    def visible(c):
        return (kpos - qpos) <= (i - c) * t

    qih, wih = [], []
    for hd in range(IDX_HEADS):
        g, e = divmod(hd, LANES // IDX_DIM)
        grp = qi_ref[0, 0, LANES * g:LANES * (g + 1), :]
        qih.append(jnp.where((feat >= IDX_DIM * e) & (feat < IDX_DIM * (e + 1)), grp, jnp.zeros_like(grp)))
        wih.append(wit_ref[0, 0, hd:hd + 1, :])

    def score_chunk(c, _):
        start = pl.multiple_of(c * t, t)
        kic = ki_ref[0, pl.ds(start, t), :]
        isc = jnp.zeros((t, t), F32)
        for hd in range(IDX_HEADS):
            isc = isc + jnp.maximum(_dot(kic, qih[hd]), 0.0) * wih[hd]
        key_ref[c] = jnp.where(visible(c), _sortable(isc), INT_MIN)
        return 0

    lax.fori_loop(0, nc, score_chunk, 0)

    def as_pattern(d):
        return lax.bitcast_convert_type(lax.shift_left(d, 16), F32).astype(BF16)

    def build_digits(shift, nbits, prefix_shift, prefix):
        def body(c, _):
            kc = key_ref[c]
            u = kc ^ INT_MIN
            member = kc != INT_MIN
            if prefix is not None:
                member = member & (lax.shift_right_logical(u, prefix_shift) == prefix)
            d = lax.shift_right_logical(u, shift) & ((1 << nbits) - 1)
            dig_ref[c] = as_pattern(jnp.where(member, d + DIGIT_BIAS, 0))
            return 0
        lax.fori_loop(0, nc, body, 0)

    one, zero = jnp.ones((), BF16), jnp.zeros((), BF16)

    def count_digits(pred):
        def body(c, accs):
            w = jnp.where(pred(dig_ref[c]), one, zero)
            accs = list(accs)
            for r in range(t // 16):
                accs[r % 2] = accs[r % 2] + w[16 * r:16 * (r + 1)]
            return tuple(accs)
        a0, a1 = lax.fori_loop(0, nc, body, (jnp.zeros((16, t), BF16),) * 2)
        return jnp.sum(a0.astype(F32) + a1.astype(F32), axis=0, keepdims=True)

    def search_digit(nbits, n_above):
        def step(b, dg):
            cand = dg | lax.shift_left(jnp.int32(1), nbits - 1 - b)
            cpat = as_pattern(cand + DIGIT_BIAS)
            cnt = n_above + count_digits(lambda x: x >= cpat)
            return jnp.where(cnt >= n_sel, cand, dg)
        dg = lax.fori_loop(0, nbits, step, jnp.zeros((1, t), jnp.int32))
        dpat = as_pattern(dg + DIGIT_BIAS)
        return dg, dpat, n_above + count_digits(lambda x: x > dpat)

    prefix, n_gt, low = None, jnp.zeros((1, t), F32), 32
    for nbits in DIGIT_BITS[:-1]:
        build_digits(low - nbits, nbits, low, prefix)
        low -= nbits
        dg, dpat, n_gt = search_digit(nbits, n_gt)
        prefix = dg if prefix is None else lax.shift_left(prefix, nbits) | dg
    n_bucket = count_digits(lambda x: x == dpat)

    def tie_cut(thr, need):
        def count(pred):
            def body(c, cnt):
                hit = pred(key_ref[c], kpos + c * t)
                return cnt + jnp.sum(jnp.where(hit, 1.0, 0.0), axis=0, keepdims=True)
            return lax.fori_loop(0, nc, body, jnp.zeros((1, t), F32))

        def cut_step(b, cut):
            cand = cut | lax.shift_left(jnp.int32(1), 10 - b)
            cnt = count(lambda kc, idx: (kc == thr) & (idx < cand))
            return jnp.where(cnt < need, cand, cut)

        nbits = max(1, int(np.ceil(np.log2(k_ref.shape[1]))))
        return lax.fori_loop(11 - nbits, 11, cut_step, jnp.zeros((1, t), jnp.int32))

    take_all = lambda: jnp.full((1, t), k_ref.shape[1], jnp.int32)
    last = DIGIT_BITS[-1]

    def split_bucket():
        build_digits(0, last, low, prefix)
        dg, dpat, n_above = search_digit(last, n_gt)
        thr = (lax.shift_left(prefix, last) | dg) ^ INT_MIN
        need = n_sel - n_above
        n_ties = count_digits(lambda x: x == dpat)
        return thr, lax.cond(jnp.max(n_ties - need) > 0.0, lambda: tie_cut(thr, need), take_all)

    def whole_bucket():
        return lax.shift_left(prefix, last) ^ INT_MIN, take_all()

    thr, cut = lax.cond(jnp.max(n_gt + n_bucket) > n_sel, split_bucket, whole_bucket)

    def bias_chunk(c, _):
        kc = key_ref[c]
        keep = visible(c) & ((kc > thr) | ((kc == thr) & (kpos + c * t <= cut)))
        bias_ref[c] = jnp.where(keep, 0.0, NEG)
        return 0

    lax.fori_loop(0, nc, bias_chunk, 0)

    qh = []
    for hd in range(DSA_HEADS):
        g, e = divmod(hd, 2)
        grp = q_ref[0, 0, LANES * g:LANES * (g + 1), :]
        qh.append(jnp.where((feat >= 64 * e) & (feat < 64 * (e + 1)), grp, jnp.zeros_like(grp)))

    def scores(c):
        start = pl.multiple_of(c * t, t)
        ks = k_ref[0, pl.ds(start, t), :]
        return [ks] * DSA_HEADS, lambda hd, r, s: s + bias_ref[c, r:r + STRIP, :]

    def step(j, carry):
        cmax, stats = carry
        kcs, finish = scores(j)
        return _pipelined_step(st_ref, pt_ref, kcs, qh, finish, cmax, stats, [vt_ref[0, j - 1]] * DSA_HEADS)

    kcs, finish = scores(0)
    init = (_score_pass(st_ref, 0, kcs, qh, finish), tuple(_softmax_init(t, DSA_DIM) for _ in range(DSA_HEADS)))
    cmax, stats = lax.fori_loop(1, nc, step, init)
    stats = _softmax_pass(st_ref, pt_ref, cmax, stats, [vt_ref[0, nc - 1]] * DSA_HEADS)
    for g in range(DSA_HEADS // 2):
        o_ref[0, :, LANES * g:LANES * (g + 1)] = _pair_output(stats[2 * g:2 * g + 2])


def _dsa(qt, k, vt, qit, ki, wit, n_sel):
    b, s, _ = k.shape
    t = ATT_T
    tile = lambda w: pl.BlockSpec((1, t, w), lambda bi, i: (bi, i, 0))
    seq = lambda w: pl.BlockSpec((1, s, w), lambda bi, i: (bi, 0, 0))
    trans = lambda r: pl.BlockSpec((1, 1, r, t), lambda bi, i: (bi, i, 0, 0))
    return pl.pallas_call(
        functools.partial(_dsa_kernel, n_sel=n_sel),
        grid=(b, s // t),
        in_specs=[trans(256), seq(LANES), pl.BlockSpec((1, s // t, DSA_DIM, t), lambda bi, i: (bi, 0, 0, 0)),
                  trans(256), seq(LANES), trans(IDX_HEADS)],
        out_specs=tile(256),
        out_shape=jax.ShapeDtypeStruct((b, s, 256), BF16),
        scratch_shapes=[pltpu.VMEM((s // t, t, t), jnp.int32), pltpu.VMEM((s // t, t, t), F32),
                        pltpu.VMEM((s // t, t, t), BF16),
                        pltpu.VMEM((1, DSA_HEADS, t, t), F32), pltpu.VMEM((DSA_HEADS, t, t), BF16)],
        compiler_params=_params("parallel", "arbitrary"),
        name="dsa",
    )(qt, k, vt, qit, ki, wit)


def _memkv_kernel(m_ref, g_ref, w_ref, k_ref, v_ref):
    hm = _rmsnorm(m_ref[0], g_ref[...]).astype(BF16)
    kv = _dot(hm, w_ref[...])
    half = kv.shape[1] // 2
    k_ref[0] = kv[:, :half].astype(BF16)
    v_ref[0] = kv[:, half:].astype(BF16)


def _memkv(mem, gain, wkv, layer):
    b, ml, d = mem.shape
    n = XA_HEADS * XA_DIM
    return pl.pallas_call(
        _memkv_kernel,
        grid=(b,),
        in_specs=[pl.BlockSpec((1, ml, d), lambda bi: (bi, 0, 0)),
                  _vec_spec(d, layer),
                  pl.BlockSpec((None, d, 2 * n), lambda bi: (layer, 0, 0))],
        out_specs=[pl.BlockSpec((1, ml, n), lambda bi: (bi, 0, 0))] * 2,
        out_shape=[jax.ShapeDtypeStruct((b, ml, n), BF16)] * 2,
        compiler_params=_params("parallel"),
        name="memkv",
    )(mem, gain, wkv)


def _mix_kernel(x_ref, a_ref, b_ref, c_ref, wout_ref, g_ref, wq_ref, kx_ref, vx_ref, wo_ref, o_ref):
    mixed = jnp.concatenate([a_ref[0], b_ref[0], c_ref[0]], axis=-1)
    x1 = x_ref[0] + _dot(mixed, wout_ref[...])
    hq = _rmsnorm(x1, g_ref[...]).astype(BF16)
    q = (_dot(hq, wq_ref[...]) * (XA_DIM ** -0.5 * LOG2E)).astype(BF16)
    outs = []
    for hd in range(XA_HEADS):
        sl = slice(XA_DIM * hd, XA_DIM * (hd + 1))
        s = _dot_nt(q[:, sl], kx_ref[0, :, sl])
        pe = jnp.exp2(s - jnp.max(s, axis=-1, keepdims=True))
        o = _dot(pe.astype(BF16), vx_ref[0, :, sl]) / jnp.sum(pe, axis=-1, keepdims=True)
        outs.append(o.astype(BF16))
    o_ref[0] = x1 + _dot(jnp.concatenate(outs, axis=-1), wo_ref[...])


def _mix_xattn(x, a, bm, c, wout, gain, wq, kx, vx, wo, layer):
    b, s, d = x.shape
    tm = MIX_TM
    n = XA_HEADS * XA_DIM
    ml = kx.shape[1]
    tile = lambda w: pl.BlockSpec((1, tm, w), lambda bi, j: (bi, j, 0))
    full = lambda r, cdim: pl.BlockSpec((None, r, cdim), lambda bi, j: (layer, 0, 0))
    mem = pl.BlockSpec((1, ml, n), lambda bi, j: (bi, 0, 0))
    return pl.pallas_call(
        _mix_kernel,
        grid=(b, s // tm),
        in_specs=[tile(d), tile(a.shape[2]), tile(bm.shape[2]), tile(c.shape[2]), full(d, d),
                  _vec_spec(d, layer), full(d, n), mem, mem, full(n, d)],
        out_specs=tile(d),
        out_shape=jax.ShapeDtypeStruct((b, s, d), F32),
        compiler_params=_params("parallel", "parallel"),
        name="mix_xattn",
    )(x, a, bm, c, wout, gain, wq, kx, vx, wo)


def _rope_tables(s, d):
    half = d // 2
    inv = ROPE_THETA ** (-jnp.arange(half, dtype=F32) / half)
    ang = jnp.arange(s).astype(F32)[:, None] * inv[None, :]
    cos, sin = jnp.cos(ang), jnp.sin(ang)
    reps = LANES // d
    return (jnp.tile(jnp.concatenate([cos, cos], -1), (1, reps)),
            jnp.tile(jnp.concatenate([-sin, sin], -1), (1, reps)))


def _layout_w_in(w_in):
    sizes = (384, 384, 384, FOX_HEADS, MLA_Q_RANK, MLA_KV_RANK, MLA_ROPE,
             DSA_HEADS * DSA_DIM, DSA_DIM, DSA_DIM, IDX_HEADS * IDX_DIM, IDX_DIM, IDX_HEADS)
    offs = np.concatenate([[0], np.cumsum(sizes)])
    fq, fk, fv, ff, mcq, mckv, mkr, dq, dk, dv, dqi, dki, dwi = (
        w_in[:, :, offs[n]:offs[n + 1]] for n in range(len(sizes)))
    zeros = lambda w: jnp.zeros(w_in.shape[:2] + (w,), w_in.dtype)
    pad = lambda a: jnp.concatenate([a, zeros(LANES - a.shape[2])], -1)
    out = jnp.concatenate([
        fq, fk, fv, pad(ff), mcq, mckv,
        jnp.concatenate([mkr, mkr, zeros(64)], -1),
        dq, jnp.concatenate([dk, dk], -1), jnp.concatenate([dv, dv], -1),
        dqi, jnp.concatenate([dki] * 4, -1), pad(dwi)], -1)
    assert out.shape[2] == N_PROJ
    return out.astype(BF16)


def _layout_w_uq(w_uq):
    hw = MLA_NOPE + MLA_ROPE
    cols = []
    for pr in range(MLA_HEADS // 2):
        h0, h1 = 2 * pr, 2 * pr + 1
        cols += [w_uq[:, :, hw * h0:hw * h0 + MLA_NOPE], w_uq[:, :, hw * h1:hw * h1 + MLA_NOPE],
                 w_uq[:, :, hw * h0 + MLA_NOPE:hw * (h0 + 1)], w_uq[:, :, hw * h1 + MLA_NOPE:hw * (h1 + 1)],
                 jnp.zeros(w_uq.shape[:2] + (64,), w_uq.dtype)]
    return jnp.concatenate(cols, -1).astype(BF16)


def _layout_w_ukv(w_ukv):
    hw = MLA_NOPE + MLA_V
    kn = [w_ukv[:, :, hw * hd:hw * hd + MLA_NOPE] for hd in range(MLA_HEADS)]
    vv = [w_ukv[:, :, hw * hd + MLA_NOPE:hw * (hd + 1)] for hd in range(MLA_HEADS)]
    return jnp.concatenate(kn + vv, -1).astype(BF16)


def kernel(x, mem, ffn1_norm, ffn1_wi, ffn1_wo, mix_norm, w_in, b_forget, mla_q_norm, mla_w_uq,
           mla_kv_norm, mla_w_ukv, w_out, xa_norm, mem_norm, xa_wq, xa_wkv, xa_wo,
           ffn2_norm, ffn2_wi, ffn2_wo, final_norm):
    b, s, d = x.shape
    depth = w_in.shape[0]
    assert s % ATT_T == 0 and s % MIX_TM == 0 and (b * s) % FFN_TM == 0 and s <= 2048
    n_sel = min(TOPK_MAX, s // 4)
    assert PROJ_TM == ATT_T and n_sel <= ATT_T

    win = _layout_w_in(w_in)
    wuq = _layout_w_uq(mla_w_uq)
    wukv = _layout_w_ukv(mla_w_ukv)
    rows = lambda v: v.reshape(-1, 1, v.shape[-1])
    bfp = rows(jnp.concatenate([b_forget, jnp.zeros((depth, LANES - FOX_HEADS), F32)], -1))
    ffn1_norm, mix_norm, mla_q_norm, mla_kv_norm, xa_norm, mem_norm, ffn2_norm, final_norm = map(
        rows, (ffn1_norm, mix_norm, mla_q_norm, mla_kv_norm, xa_norm, mem_norm, ffn2_norm, final_norm))
    tabs = _rope_tables(s, IDX_DIM) + _rope_tables(s, DSA_DIM)
    w1i, w1o = ffn1_wi.astype(BF16), ffn1_wo.astype(BF16)
    w2i, w2o = ffn2_wi.astype(BF16), ffn2_wo.astype(BF16)
    wout, wq, wkv, wo = (w.astype(BF16) for w in (w_out, xa_wq, xa_wkv, xa_wo))

    for layer in range(depth):
        x = _ffn(x.reshape(b * s, d), ffn1_norm, w1i, w1o, layer).reshape(b, s, d)
        (fq, fk, fv, cq, ck, mq, mk, mv, dq, dk, dv, qi, ki, wi) = _proj(
            x, mix_norm, win, bfp, mla_q_norm, wuq, mla_kv_norm, wukv, tabs, layer)
        a_out = _attn(fq, fk, fv, cq, ck, kw=LANES, name="fox_attn")
        b_out = _attn(mq, mk, mv, kw=2 * LANES, name="mla_attn")
        c_out = _dsa(dq, dk, dv, qi, ki, wi, n_sel)
        kx, vx = _memkv(mem, mem_norm, wkv, layer)
        x = _mix_xattn(x, a_out, b_out, c_out, wout, xa_norm, wq, kx, vx, wo, layer)
        last = layer == depth - 1
        x = _ffn(x.reshape(b * s, d), ffn2_norm, w2i, w2o, layer,
                 final_gain=final_norm if last else None).reshape(b, s, d)
    return x
```

```python
import functools

import jax
import jax.numpy as jnp
import numpy as np
from jax import lax
from jax.experimental import pallas as pl
from jax.experimental.pallas import tpu as pltpu

F32 = jnp.float32
BF16 = jnp.bfloat16

ROPE_THETA = 10000.0
NORM_EPS = 1e-6
FOX_HEADS, FOX_DIM = 6, 64
MLA_HEADS, MLA_NOPE, MLA_ROPE, MLA_V = 6, 64, 32, 64
MLA_Q_RANK, MLA_KV_RANK = 256, 128
DSA_HEADS, DSA_DIM = 4, 64
IDX_HEADS, IDX_DIM = 8, 32
TOPK_MAX = 256
XA_HEADS, XA_DIM = 4, 128

LANES = 128
VMEM_LIMIT = 56 * 1024 * 1024
NEG = -1e30
LOG2E = 1.4426950408889634
INT_MIN = -(2 ** 31)
DIGIT_BITS = (14, 14, 4)
DIGIT_BIAS = 0x80

O_FQ, O_FK, O_FV, O_FF = 0, 384, 768, 1152
O_MCQ, O_MCKV, O_MKR = 1280, 1536, 1664
O_DQ, O_DK, O_DV, O_DQI, O_DKI, O_DWI = 1792, 2048, 2176, 2304, 2560, 2688
N_PROJ = 2816

FFN_TM, FFN_TF = 1024, 256
PROJ_TM = 256
ATT_T = 256
STRIP = 64
MIX_TM = 512


def _dot(a, b):
    return jnp.dot(a, b, preferred_element_type=F32)


def _dot_nt(a, b):
    return lax.dot_general(a, b, (((1,), (1,)), ((), ())), preferred_element_type=F32)


def _rmsnorm(x, g):
    return x * lax.rsqrt(jnp.mean(x * x, axis=-1, keepdims=True) + NORM_EPS) * g


def _vec_spec(width, layer):
    return pl.BlockSpec((None, 1, width), lambda *_: (layer, 0, 0))


def _params(*sem):
    return pltpu.CompilerParams(dimension_semantics=sem, vmem_limit_bytes=VMEM_LIMIT)


def _ffn_kernel(x_ref, g_ref, wi_ref, wo_ref, fg_ref, o_ref, h_ref, act_ref, *, final_norm):
    f = wo_ref.shape[0]
    h_ref[...] = _rmsnorm(x_ref[...], g_ref[...]).astype(BF16)
    for j in range(f // FFN_TF):
        lo, hi = j * FFN_TF, (j + 1) * FFN_TF
        gate = _dot(h_ref[...], wi_ref[:, lo:hi])
        up = _dot(h_ref[...], wi_ref[:, f + lo:f + hi])
        act_ref[:, lo:hi] = (gate * jax.nn.sigmoid(gate) * up).astype(BF16)
    y = x_ref[...] + 0.5 * _dot(act_ref[...], wo_ref[...])
    if final_norm:
        y = _rmsnorm(y, fg_ref[...])
    o_ref[...] = y


def _ffn(x2d, gain, wi, wo, layer, final_gain=None):
    m, d = x2d.shape
    f = wo.shape[1]
    assert f % FFN_TF == 0
    fg = gain if final_gain is None else final_gain
    resident = lambda r, c: pl.BlockSpec((None, r, c), lambda i: (layer, 0, 0), pipeline_mode=pl.Buffered(1))
    return pl.pallas_call(
        functools.partial(_ffn_kernel, final_norm=final_gain is not None),
        grid=(m // FFN_TM,),
        in_specs=[
            pl.BlockSpec((FFN_TM, d), lambda i: (i, 0)),
            _vec_spec(d, layer),
            resident(d, 2 * f),
            resident(f, d),
            _vec_spec(d, 0 if final_gain is not None else layer),
        ],
        out_specs=pl.BlockSpec((FFN_TM, d), lambda i: (i, 0)),
        out_shape=jax.ShapeDtypeStruct((m, d), F32),
        scratch_shapes=[pltpu.VMEM((FFN_TM, d), BF16), pltpu.VMEM((FFN_TM, f), BF16)],
        compiler_params=_params("parallel"),
        name="ffn",
    )(x2d, gain, wi, wo, fg)


def _rope128(x, cos, sin, half):
    lane = lax.broadcasted_iota(jnp.int32, x.shape, 1)
    first = (lane % (2 * half)) < half
    partner = jnp.where(first, pltpu.roll(x, LANES - half, 1), pltpu.roll(x, half, 1))
    return x * cos + partner * sin


def _proj_kernel(x_ref, g_ref, win_ref, bf_ref, gq_ref, wuq_ref, gkv_ref, wukv_ref,
                 c32_ref, s32_ref, c64_ref, s64_ref,
                 fq_ref, fk_ref, fv_ref, cq_ref, ck_ref,
                 mq_ref, mk_ref, mv_ref,
                 dq_ref, dk_ref, dv_ref, qi_ref, ki_ref, wi_ref,
                 carry_ref):
    j = pl.program_id(1)
    tm = x_ref.shape[1]

    @pl.when(j == 0)
    def _():
        carry_ref[...] = jnp.zeros_like(carry_ref)

    h = _rmsnorm(x_ref[0], g_ref[...]).astype(BF16)
    p = _dot(h, win_ref[...])

    def grp(off, width=LANES):
        return p[:, off:off + width]

    c32, s32 = c32_ref[...], s32_ref[...]
    c64, s64 = c64_ref[...], s64_ref[...]

    fq_ref[0, 0] = (grp(O_FQ, 384) * (FOX_DIM ** -0.5 * LOG2E)).T.astype(BF16)
    fk_ref[0] = grp(O_FK, 384).astype(BF16)
    fv_ref[0, 0] = grp(O_FV, 384).T.astype(BF16)

    z = grp(O_FF) + bf_ref[...]
    logf = jnp.minimum(z, 0.0) - jnp.log(1.0 + jnp.exp(-jnp.abs(z)))
    hi = logf.astype(BF16)
    r1 = logf - hi.astype(F32)
    mid = r1.astype(BF16)
    lo = (r1 - mid.astype(F32)).astype(BF16)
    row = lax.broadcasted_iota(jnp.int32, (tm, tm), 0)
    col = lax.broadcasted_iota(jnp.int32, (tm, tm), 1)
    tri = jnp.where(row >= col, 1.0, 0.0).astype(BF16)
    cum = (_dot(tri, hi) + _dot(tri, mid)) + _dot(tri, lo) + carry_ref[0:1, :]
    carry_ref[0:1, :] = cum[tm - 1:tm, :]
    cum2 = cum * LOG2E
    cq_ref[0] = cum2
    ck_ref[0, 0] = cum2.T[0:8, :]

    qn = _rmsnorm(grp(O_MCQ, MLA_Q_RANK), gq_ref[...]).astype(BF16)
    qm = _dot(qn, wuq_ref[...]) * ((MLA_NOPE + MLA_ROPE) ** -0.5 * LOG2E)
    kvn = _rmsnorm(grp(O_MCKV, MLA_KV_RANK), gkv_ref[...]).astype(BF16)
    kvm = _dot(kvn, wukv_ref[...])
    krt = _rope128(grp(O_MKR), c32, s32, MLA_ROPE // 2).astype(BF16)
    for pr in range(MLA_HEADS // 2):
        mq_ref[0, 0, 256 * pr:256 * pr + 128, :] = qm[:, 256 * pr:256 * pr + 128].T.astype(BF16)
        mq_ref[0, 0, 256 * pr + 128:256 * pr + 256, :] = _rope128(
            qm[:, 256 * pr + 128:256 * pr + 256], c32, s32, MLA_ROPE // 2).T.astype(BF16)
        mk_ref[0, :, 256 * pr:256 * pr + 128] = kvm[:, 128 * pr:128 * pr + 128].astype(BF16)
        mk_ref[0, :, 256 * pr + 128:256 * pr + 256] = krt
    mv_ref[0, 0] = kvm[:, 384:768].T.astype(BF16)

    for gi in range(2):
        dq_ref[0, 0, 128 * gi:128 * gi + 128, :] = (
            _rope128(grp(O_DQ + 128 * gi), c64, s64, DSA_DIM // 2) * (DSA_DIM ** -0.5 * LOG2E)).T.astype(BF16)
        qi_ref[0, 0, 128 * gi:128 * gi + 128, :] = (
            _rope128(grp(O_DQI + 128 * gi), c32, s32, IDX_DIM // 2) * (IDX_DIM ** -0.5)).T.astype(BF16)
    dk_ref[0] = _rope128(grp(O_DK), c64, s64, DSA_DIM // 2).astype(BF16)
    dv_ref[0, 0] = grp(O_DV).T[0:DSA_DIM, :].astype(BF16)
    ki_ref[0] = _rope128(grp(O_DKI), c32, s32, IDX_DIM // 2).astype(BF16)
    wi_ref[0, 0] = (grp(O_DWI) * (IDX_HEADS ** -0.5)).T[0:IDX_HEADS, :]


def _proj(x, gain, win, bfp, gq, wuq, gkv, wukv, tabs, layer):
    b, s, d = x.shape
    tm = PROJ_TM
    nt = s // tm
    tok = lambda w, dt: jax.ShapeDtypeStruct((b, s, w), dt)
    tspec = lambda w: pl.BlockSpec((1, tm, w), lambda bi, j: (bi, j, 0))
    full = lambda shape: pl.BlockSpec((None,) + shape, lambda bi, j: (layer,) + (0,) * len(shape))
    vec = lambda w: _vec_spec(w, layer)
    tab = pl.BlockSpec((tm, LANES), lambda bi, j: (j, 0))
    tr = lambda r, dt: jax.ShapeDtypeStruct((b, nt, r, tm), dt)
    trspec = lambda r: pl.BlockSpec((1, 1, r, tm), lambda bi, j: (bi, j, 0, 0))
    out_shape = [tr(384, BF16), tok(384, BF16), tr(384, BF16), tok(LANES, F32), tr(8, F32),
                 tr(768, BF16), tok(768, BF16), tr(384, BF16),
                 tr(256, BF16), tok(LANES, BF16), tr(DSA_DIM, BF16),
                 tr(256, BF16), tok(LANES, BF16), tr(IDX_HEADS, F32)]
    out_specs = [trspec(384), tspec(384), trspec(384), tspec(LANES), trspec(8),
                 trspec(768), tspec(768), trspec(384),
                 trspec(256), tspec(LANES), trspec(DSA_DIM),
                 trspec(256), tspec(LANES), trspec(IDX_HEADS)]
    return pl.pallas_call(
        _proj_kernel,
        grid=(b, nt),
        in_specs=[tspec(d), vec(d), full((d, N_PROJ)), vec(LANES),
                  vec(MLA_Q_RANK), full((MLA_Q_RANK, 768)), vec(MLA_KV_RANK), full((MLA_KV_RANK, 768)),
                  tab, tab, tab, tab],
        out_specs=out_specs,
        out_shape=out_shape,
        scratch_shapes=[pltpu.VMEM((8, LANES), F32)],
        compiler_params=_params("arbitrary", "arbitrary"),
        name="proj",
    )(x, gain, win, bfp, gq, wuq, gkv, wukv, *tabs)


def _score_pass(st_ref, slot, k_chunks, qh, finish):
    sts = [_dot(kc, q) for kc, q in zip(k_chunks, qh)]
    cmax = []
    for hd, st in enumerate(sts):
        mx = None
        for r in range(0, st.shape[0], STRIP):
            s = finish(hd, r, st[r:r + STRIP])
            st_ref[slot, hd, r:r + STRIP, :] = s
            pm = jnp.max(s, axis=0, keepdims=True)
            mx = pm if mx is None else jnp.maximum(mx, pm)
        cmax.append(mx)
    return tuple(cmax)


def _probabilities(st_ref, pt_ref, hd, m, cm):
    m_new = jnp.maximum(m, cm)
    for r in range(0, st_ref.shape[2], STRIP):
        pt_ref[hd, r:r + STRIP, :] = jnp.exp2(st_ref[0, hd, r:r + STRIP, :] - m_new).astype(BF16)
    return m_new, jnp.exp2(m - m_new)


def _accumulate(pt_ref, pre, stats, vts):
    ones = jnp.ones((16, pt_ref.shape[1]), BF16)
    pvs = [_dot(jnp.concatenate([vt, ones], axis=0), pt_ref[hd]) for hd, vt in enumerate(vts)]
    out = []
    for (m_new, alpha), (_, l, acc), pv in zip(pre, stats, pvs):
        dv = acc.shape[0]
        out.append((m_new, alpha * l + pv[dv:dv + 1], alpha * acc + pv[:dv]))
    return tuple(out)


def _softmax_pass(st_ref, pt_ref, cmax, stats, vts):
    pre = [_probabilities(st_ref, pt_ref, hd, m, cm) for hd, (cm, (m, _, _)) in enumerate(zip(cmax, stats))]
    return _accumulate(pt_ref, pre, stats, vts)


def _pipelined_step(st_ref, pt_ref, k_chunks, qh, finish, cmax, stats, vts):
    pre, nxt = [], []
    for hd, (kc, q) in enumerate(zip(k_chunks, qh)):
        st_new = _dot(kc, q)
        pre.append(_probabilities(st_ref, pt_ref, hd, stats[hd][0], cmax[hd]))
        mx = None
        for r in range(0, st_new.shape[0], STRIP):
            s = finish(hd, r, st_new[r:r + STRIP])
            st_ref[0, hd, r:r + STRIP, :] = s
            pm = jnp.max(s, axis=0, keepdims=True)
            mx = pm if mx is None else jnp.maximum(mx, pm)
        nxt.append(mx)
    return tuple(nxt), _accumulate(pt_ref, pre, stats, vts)


def _softmax_init(t, dv):
    return (jnp.full((1, t), NEG, F32), jnp.zeros((1, t), F32), jnp.zeros((dv, t), F32))


def _pair_output(carry):
    ot = jnp.concatenate([carry[0][2] / carry[0][1], carry[1][2] / carry[1][1]], axis=0)
    return ot.T.astype(BF16)


def _attn_kernel(*refs, n_pairs, kw, has_bias):
    if has_bias:
        q_ref, k_ref, vt_ref, ccol_ref, crow_ref, o_ref, st_ref, pt_ref = refs
    else:
        q_ref, k_ref, vt_ref, o_ref, st_ref, pt_ref = refs
    i = pl.program_id(1)
    t = q_ref.shape[3]
    feat = lax.broadcasted_iota(jnp.int32, (kw, t), 0)
    kpos = lax.broadcasted_iota(jnp.int32, (STRIP, t), 0)
    qpos = lax.broadcasted_iota(jnp.int32, (STRIP, t), 1)

    qh = []
    for pr in range(n_pairs):
        qs = q_ref[0, 0, kw * pr:kw * (pr + 1), :]
        for e in range(2):
            sel = (feat >= 64 * e) & (feat < 64 * e + 64)
            if kw == 256:
                sel = sel | ((feat >= 128 + 32 * e) & (feat < 160 + 32 * e))
            qh.append(jnp.where(sel, qs, jnp.zeros_like(qs)))

    heads = range(2 * n_pairs)

    def chunk(c, diag):
        start = pl.multiple_of(c * t, t)

        def finish(hd, r, s):
            if has_bias:
                s = s + (crow_ref[0, 0, hd:hd + 1, :] - ccol_ref[0, pl.ds(start + r, STRIP), hd:hd + 1])
            if diag:
                s = jnp.where(kpos + r <= qpos, s, NEG)
            return s

        return [k_ref[0, pl.ds(start, t), kw * (hd // 2):kw * (hd // 2 + 1)] for hd in heads], finish

    def values(c):
        return [vt_ref[0, c, 64 * hd:64 * (hd + 1), :] for hd in heads]

    def step(j, carry):
        cmax, stats = carry
        kcs, finish = chunk(j, False)
        return _pipelined_step(st_ref, pt_ref, kcs, qh, finish, cmax, stats,
                               values(jnp.where(j == 0, i, j - 1)))

    kcs, finish = chunk(i, True)
    init = (_score_pass(st_ref, 0, kcs, qh, finish), tuple(_softmax_init(t, 64) for _ in heads))
    cmax, stats = lax.fori_loop(0, i, step, init)
    stats = _softmax_pass(st_ref, pt_ref, cmax, stats, values(jnp.maximum(i - 1, 0)))
    for pr in range(n_pairs):
        o_ref[0, :, LANES * pr:LANES * (pr + 1)] = _pair_output(stats[2 * pr:2 * pr + 2])


def _attn(qt, k, vt, ccol=None, crow=None, *, kw, name):
    b, s, qw = k.shape
    n_pairs = qw // kw
    t = ATT_T
    has_bias = ccol is not None
    in_specs = [pl.BlockSpec((1, 1, qw, t), lambda bi, i: (bi, i, 0, 0)),
                pl.BlockSpec((1, s, qw), lambda bi, i: (bi, 0, 0)),
                pl.BlockSpec((1, s // t, n_pairs * LANES, t), lambda bi, i: (bi, 0, 0, 0))]
    args = [qt, k, vt]
    if has_bias:
        in_specs += [pl.BlockSpec((1, s, LANES), lambda bi, i: (bi, 0, 0)),
                     pl.BlockSpec((1, 1, 8, t), lambda bi, i: (bi, i, 0, 0))]
        args += [ccol, crow]
    return pl.pallas_call(
        functools.partial(_attn_kernel, n_pairs=n_pairs, kw=kw, has_bias=has_bias),
        grid=(b, s // t),
        in_specs=in_specs,
        out_specs=pl.BlockSpec((1, t, n_pairs * LANES), lambda bi, i: (bi, i, 0)),
        out_shape=jax.ShapeDtypeStruct((b, s, n_pairs * LANES), BF16),
        scratch_shapes=[pltpu.VMEM((1, 2 * n_pairs, t, t), F32), pltpu.VMEM((2 * n_pairs, t, t), BF16)],
        compiler_params=_params("parallel", "arbitrary"),
        name=name,
    )(*args)


def _sortable(x):
    bits = lax.bitcast_convert_type(x + 0.0, jnp.int32)
    return jnp.where(bits < 0, bits ^ 0x7FFFFFFF, bits)


def _dsa_kernel(q_ref, k_ref, vt_ref, qi_ref, ki_ref, wit_ref, o_ref, key_ref, bias_ref, dig_ref, st_ref, pt_ref, *, n_sel):
    i = pl.program_id(1)
    t = q_ref.shape[3]
    nc = i + 1
    feat = lax.broadcasted_iota(jnp.int32, (LANES, t), 0)
    kpos = lax.broadcasted_iota(jnp.int32, (t, t), 0)
    qpos = lax.broadcasted_iota(jnp.int32, (t, t), 1)

    def visible(c):
        return (kpos - qpos) <= (i - c) * t

    qih, wih = [], []
    for hd in range(IDX_HEADS):
        g, e = divmod(hd, LANES // IDX_DIM)
        grp = qi_ref[0, 0, LANES * g:LANES * (g + 1), :]
        qih.append(jnp.where((feat >= IDX_DIM * e) & (feat < IDX_DIM * (e + 1)), grp, jnp.zeros_like(grp)))
        wih.append(wit_ref[0, 0, hd:hd + 1, :])

    def score_chunk(c, _):
        start = pl.multiple_of(c * t, t)
        kic = ki_ref[0, pl.ds(start, t), :]
        isc = jnp.zeros((t, t), F32)
        for hd in range(IDX_HEADS):
            isc = isc + jnp.maximum(_dot(kic, qih[hd]), 0.0) * wih[hd]
        key_ref[c] = jnp.where(visible(c), _sortable(isc), INT_MIN)
        return 0

    lax.fori_loop(0, nc, score_chunk, 0)

    def as_pattern(d):
        return lax.bitcast_convert_type(lax.shift_left(d, 16), F32).astype(BF16)

    def build_digits(shift, nbits, prefix_shift, prefix):
        def body(c, _):
            kc = key_ref[c]
            u = kc ^ INT_MIN
            member = kc != INT_MIN
            if prefix is not None:
                member = member & (lax.shift_right_logical(u, prefix_shift) == prefix)
            d = lax.shift_right_logical(u, shift) & ((1 << nbits) - 1)
            dig_ref[c] = as_pattern(jnp.where(member, d + DIGIT_BIAS, 0))
            return 0
        lax.fori_loop(0, nc, body, 0)

    one, zero = jnp.ones((), BF16), jnp.zeros((), BF16)

    def count_digits(pred):
        def body(c, accs):
            w = jnp.where(pred(dig_ref[c]), one, zero)
            accs = list(accs)
            for r in range(t // 16):
                accs[r % 2] = accs[r % 2] + w[16 * r:16 * (r + 1)]
            return tuple(accs)
        a0, a1 = lax.fori_loop(0, nc, body, (jnp.zeros((16, t), BF16),) * 2)
        return jnp.sum(a0.astype(F32) + a1.astype(F32), axis=0, keepdims=True)

    def search_digit(nbits, n_above):
        def step(b, dg):
            cand = dg | lax.shift_left(jnp.int32(1), nbits - 1 - b)
            cpat = as_pattern(cand + DIGIT_BIAS)
            cnt = n_above + count_digits(lambda x: x >= cpat)
            return jnp.where(cnt >= n_sel, cand, dg)
        dg = lax.fori_loop(0, nbits, step, jnp.zeros((1, t), jnp.int32))
        dpat = as_pattern(dg + DIGIT_BIAS)
        return dg, dpat, n_above + count_digits(lambda x: x > dpat)

    def tie_cut(thr, need):
        def count(pred):
            def body(c, cnt):
                hit = pred(key_ref[c], kpos + c * t)
                return cnt + jnp.sum(jnp.where(hit, 1.0, 0.0), axis=0, keepdims=True)
            return lax.fori_loop(0, nc, body, jnp.zeros((1, t), F32))

        def cut_step(b, cut):
            cand = cut | lax.shift_left(jnp.int32(1), 10 - b)
            cnt = count(lambda kc, idx: (kc == thr) & (idx < cand))
            return jnp.where(cnt < need, cand, cut)

        nbits = max(1, int(np.ceil(np.log2(k_ref.shape[1]))))
        return lax.fori_loop(11 - nbits, 11, cut_step, jnp.zeros((1, t), jnp.int32))

    take_all = lambda: jnp.full((1, t), k_ref.shape[1], jnp.int32)
    last = DIGIT_BITS[-1]

    def select():
        prefix, n_gt, low = None, jnp.zeros((1, t), F32), 32
        for nbits in DIGIT_BITS[:-1]:
            build_digits(low - nbits, nbits, low, prefix)
            low -= nbits
            dg, dpat, n_gt = search_digit(nbits, n_gt)
            prefix = dg if prefix is None else lax.shift_left(prefix, nbits) | dg
        n_bucket = count_digits(lambda x: x == dpat)

        def split_bucket():
            build_digits(0, last, low, prefix)
            dg, dpat, n_above = search_digit(last, n_gt)
            thr = (lax.shift_left(prefix, last) | dg) ^ INT_MIN
            need = n_sel - n_above
            n_ties = count_digits(lambda x: x == dpat)
            return thr, lax.cond(jnp.max(n_ties - need) > 0.0, lambda: tie_cut(thr, need), take_all)

        def whole_bucket():
            return lax.shift_left(prefix, last) ^ INT_MIN, take_all()

        return lax.cond(jnp.max(n_gt + n_bucket) > n_sel, split_bucket, whole_bucket)

    def take_everything():
        return jnp.full((1, t), INT_MIN, jnp.int32), take_all()

    thr, cut = lax.cond(nc * t > n_sel, select, take_everything)

    def bias_chunk(c, _):
        kc = key_ref[c]
        keep = visible(c) & ((kc > thr) | ((kc == thr) & (kpos + c * t <= cut)))
        bias_ref[c] = jnp.where(keep, 0.0, NEG)
        return 0

    lax.fori_loop(0, nc, bias_chunk, 0)

    qh = []
    for hd in range(DSA_HEADS):
        g, e = divmod(hd, 2)
        grp = q_ref[0, 0, LANES * g:LANES * (g + 1), :]
        qh.append(jnp.where((feat >= 64 * e) & (feat < 64 * (e + 1)), grp, jnp.zeros_like(grp)))

    def scores(c):
        start = pl.multiple_of(c * t, t)
        ks = k_ref[0, pl.ds(start, t), :]
        return [ks] * DSA_HEADS, lambda hd, r, s: s + bias_ref[c, r:r + STRIP, :]

    def step(j, carry):
        cmax, stats = carry
        kcs, finish = scores(j)
        return _pipelined_step(st_ref, pt_ref, kcs, qh, finish, cmax, stats, [vt_ref[0, j - 1]] * DSA_HEADS)

    kcs, finish = scores(0)
    init = (_score_pass(st_ref, 0, kcs, qh, finish), tuple(_softmax_init(t, DSA_DIM) for _ in range(DSA_HEADS)))
    cmax, stats = lax.fori_loop(1, nc, step, init)
    stats = _softmax_pass(st_ref, pt_ref, cmax, stats, [vt_ref[0, nc - 1]] * DSA_HEADS)
    for g in range(DSA_HEADS // 2):
        o_ref[0, :, LANES * g:LANES * (g + 1)] = _pair_output(stats[2 * g:2 * g + 2])


def _dsa(qt, k, vt, qit, ki, wit, n_sel):
    b, s, _ = k.shape
    t = ATT_T
    tile = lambda w: pl.BlockSpec((1, t, w), lambda bi, i: (bi, i, 0))
    seq = lambda w: pl.BlockSpec((1, s, w), lambda bi, i: (bi, 0, 0))
    trans = lambda r: pl.BlockSpec((1, 1, r, t), lambda bi, i: (bi, i, 0, 0))
    return pl.pallas_call(
        functools.partial(_dsa_kernel, n_sel=n_sel),
        grid=(b, s // t),
        in_specs=[trans(256), seq(LANES), pl.BlockSpec((1, s // t, DSA_DIM, t), lambda bi, i: (bi, 0, 0, 0)),
                  trans(256), seq(LANES), trans(IDX_HEADS)],
        out_specs=tile(256),
        out_shape=jax.ShapeDtypeStruct((b, s, 256), BF16),
        scratch_shapes=[pltpu.VMEM((s // t, t, t), jnp.int32), pltpu.VMEM((s // t, t, t), F32),
                        pltpu.VMEM((s // t, t, t), BF16),
                        pltpu.VMEM((1, DSA_HEADS, t, t), F32), pltpu.VMEM((DSA_HEADS, t, t), BF16)],
        compiler_params=_params("parallel", "arbitrary"),
        name="dsa",
    )(qt, k, vt, qit, ki, wit)


def _memkv_kernel(m_ref, g_ref, w_ref, k_ref, v_ref):
    hm = _rmsnorm(m_ref[0], g_ref[...]).astype(BF16)
    kv = _dot(hm, w_ref[...])
    half = kv.shape[1] // 2
    k_ref[0] = kv[:, :half].astype(BF16)
    v_ref[0] = kv[:, half:].astype(BF16)


def _memkv(mem, gain, wkv, layer):
    b, ml, d = mem.shape
    n = XA_HEADS * XA_DIM
    return pl.pallas_call(
        _memkv_kernel,
        grid=(b,),
        in_specs=[pl.BlockSpec((1, ml, d), lambda bi: (bi, 0, 0)),
                  _vec_spec(d, layer),
                  pl.BlockSpec((None, d, 2 * n), lambda bi: (layer, 0, 0))],
        out_specs=[pl.BlockSpec((1, ml, n), lambda bi: (bi, 0, 0))] * 2,
        out_shape=[jax.ShapeDtypeStruct((b, ml, n), BF16)] * 2,
        compiler_params=_params("parallel"),
        name="memkv",
    )(mem, gain, wkv)


def _mix_kernel(x_ref, a_ref, b_ref, c_ref, wout_ref, g_ref, wq_ref, kx_ref, vx_ref, wo_ref, o_ref):
    mixed = jnp.concatenate([a_ref[0], b_ref[0], c_ref[0]], axis=-1)
    x1 = x_ref[0] + _dot(mixed, wout_ref[...])
    hq = _rmsnorm(x1, g_ref[...]).astype(BF16)
    q = (_dot(hq, wq_ref[...]) * (XA_DIM ** -0.5 * LOG2E)).astype(BF16)
    outs = []
    for hd in range(XA_HEADS):
        sl = slice(XA_DIM * hd, XA_DIM * (hd + 1))
        s = _dot_nt(q[:, sl], kx_ref[0, :, sl])
        pe = jnp.exp2(s - jnp.max(s, axis=-1, keepdims=True))
        o = _dot(pe.astype(BF16), vx_ref[0, :, sl]) / jnp.sum(pe, axis=-1, keepdims=True)
        outs.append(o.astype(BF16))
    o_ref[0] = x1 + _dot(jnp.concatenate(outs, axis=-1), wo_ref[...])


def _mix_xattn(x, a, bm, c, wout, gain, wq, kx, vx, wo, layer):
    b, s, d = x.shape
    tm = MIX_TM
    n = XA_HEADS * XA_DIM
    ml = kx.shape[1]
    tile = lambda w: pl.BlockSpec((1, tm, w), lambda bi, j: (bi, j, 0))
    full = lambda r, cdim: pl.BlockSpec((None, r, cdim), lambda bi, j: (layer, 0, 0))
    mem = pl.BlockSpec((1, ml, n), lambda bi, j: (bi, 0, 0))
    return pl.pallas_call(
        _mix_kernel,
        grid=(b, s // tm),
        in_specs=[tile(d), tile(a.shape[2]), tile(bm.shape[2]), tile(c.shape[2]), full(d, d),
                  _vec_spec(d, layer), full(d, n), mem, mem, full(n, d)],
        out_specs=tile(d),
        out_shape=jax.ShapeDtypeStruct((b, s, d), F32),
        compiler_params=_params("parallel", "parallel"),
        name="mix_xattn",
    )(x, a, bm, c, wout, gain, wq, kx, vx, wo)


def _rope_tables(s, d):
    half = d // 2
    inv = ROPE_THETA ** (-jnp.arange(half, dtype=F32) / half)
    ang = jnp.arange(s).astype(F32)[:, None] * inv[None, :]
    cos, sin = jnp.cos(ang), jnp.sin(ang)
    reps = LANES // d
    return (jnp.tile(jnp.concatenate([cos, cos], -1), (1, reps)),
            jnp.tile(jnp.concatenate([-sin, sin], -1), (1, reps)))


def _layout_w_in(w_in):
    sizes = (384, 384, 384, FOX_HEADS, MLA_Q_RANK, MLA_KV_RANK, MLA_ROPE,
             DSA_HEADS * DSA_DIM, DSA_DIM, DSA_DIM, IDX_HEADS * IDX_DIM, IDX_DIM, IDX_HEADS)
    offs = np.concatenate([[0], np.cumsum(sizes)])
    fq, fk, fv, ff, mcq, mckv, mkr, dq, dk, dv, dqi, dki, dwi = (
        w_in[:, :, offs[n]:offs[n + 1]] for n in range(len(sizes)))
    zeros = lambda w: jnp.zeros(w_in.shape[:2] + (w,), w_in.dtype)
    pad = lambda a: jnp.concatenate([a, zeros(LANES - a.shape[2])], -1)
    out = jnp.concatenate([
        fq, fk, fv, pad(ff), mcq, mckv,
        jnp.concatenate([mkr, mkr, zeros(64)], -1),
        dq, jnp.concatenate([dk, dk], -1), jnp.concatenate([dv, dv], -1),
        dqi, jnp.concatenate([dki] * 4, -1), pad(dwi)], -1)
    assert out.shape[2] == N_PROJ
    return out.astype(BF16)


def _layout_w_uq(w_uq):
    hw = MLA_NOPE + MLA_ROPE
    cols = []
    for pr in range(MLA_HEADS // 2):
        h0, h1 = 2 * pr, 2 * pr + 1
        cols += [w_uq[:, :, hw * h0:hw * h0 + MLA_NOPE], w_uq[:, :, hw * h1:hw * h1 + MLA_NOPE],
                 w_uq[:, :, hw * h0 + MLA_NOPE:hw * (h0 + 1)], w_uq[:, :, hw * h1 + MLA_NOPE:hw * (h1 + 1)],
                 jnp.zeros(w_uq.shape[:2] + (64,), w_uq.dtype)]
    return jnp.concatenate(cols, -1).astype(BF16)


def _layout_w_ukv(w_ukv):
    hw = MLA_NOPE + MLA_V
    kn = [w_ukv[:, :, hw * hd:hw * hd + MLA_NOPE] for hd in range(MLA_HEADS)]
    vv = [w_ukv[:, :, hw * hd + MLA_NOPE:hw * (hd + 1)] for hd in range(MLA_HEADS)]
    return jnp.concatenate(kn + vv, -1).astype(BF16)


def kernel(x, mem, ffn1_norm, ffn1_wi, ffn1_wo, mix_norm, w_in, b_forget, mla_q_norm, mla_w_uq,
           mla_kv_norm, mla_w_ukv, w_out, xa_norm, mem_norm, xa_wq, xa_wkv, xa_wo,
           ffn2_norm, ffn2_wi, ffn2_wo, final_norm):
    b, s, d = x.shape
    depth = w_in.shape[0]
    assert s % ATT_T == 0 and s % MIX_TM == 0 and (b * s) % FFN_TM == 0 and s <= 2048
    n_sel = min(TOPK_MAX, s // 4)
    assert PROJ_TM == ATT_T and n_sel <= ATT_T

    win = _layout_w_in(w_in)
    wuq = _layout_w_uq(mla_w_uq)
    wukv = _layout_w_ukv(mla_w_ukv)
    rows = lambda v: v.reshape(-1, 1, v.shape[-1])
    bfp = rows(jnp.concatenate([b_forget, jnp.zeros((depth, LANES - FOX_HEADS), F32)], -1))
    ffn1_norm, mix_norm, mla_q_norm, mla_kv_norm, xa_norm, mem_norm, ffn2_norm, final_norm = map(
        rows, (ffn1_norm, mix_norm, mla_q_norm, mla_kv_norm, xa_norm, mem_norm, ffn2_norm, final_norm))
    tabs = _rope_tables(s, IDX_DIM) + _rope_tables(s, DSA_DIM)
    w1i, w1o = ffn1_wi.astype(BF16), ffn1_wo.astype(BF16)
    w2i, w2o = ffn2_wi.astype(BF16), ffn2_wo.astype(BF16)
    wout, wq, wkv, wo = (w.astype(BF16) for w in (w_out, xa_wq, xa_wkv, xa_wo))

    for layer in range(depth):
        x = _ffn(x.reshape(b * s, d), ffn1_norm, w1i, w1o, layer).reshape(b, s, d)
        (fq, fk, fv, cq, ck, mq, mk, mv, dq, dk, dv, qi, ki, wi) = _proj(
            x, mix_norm, win, bfp, mla_q_norm, wuq, mla_kv_norm, wukv, tabs, layer)
        a_out = _attn(fq, fk, fv, cq, ck, kw=LANES, name="fox_attn")
        b_out = _attn(mq, mk, mv, kw=2 * LANES, name="mla_attn")
        c_out = _dsa(dq, dk, dv, qi, ki, wi, n_sel)
        kx, vx = _memkv(mem, mem_norm, wkv, layer)
        x = _mix_xattn(x, a_out, b_out, c_out, wout, xa_norm, wq, kx, vx, wo, layer)
        last = layer == depth - 1
        x = _ffn(x.reshape(b * s, d), ffn2_norm, w2i, w2o, layer,
                 final_gain=final_norm if last else None).reshape(b, s, d)
    return x
```
